```python
import math
import jax, jax.numpy as jnp
from jax import lax
import numpy as np

D_MODEL = 1024
BATCH = 8
SEQ = 4096
DEPTH = 1
DEC_BATCH = 128
DEC_SEQ = 8
PAST_LEN = 8192
PAGE_SIZE = 128

SSM_WIDTH = D_MODEL
SSM_GROUP = 16
SSM_GROUPS = SSM_WIDTH // SSM_GROUP
SSM_STATE = 64
DT_MIN = 1e-3
DT_MAX = 1e-1
N_HEADS = 8
HEAD_DIM = 128
ATT_WIDTH = N_HEADS * HEAD_DIM
MOBA_BLOCK = 256
MOBA_TOPK = 3
Q_CHUNK = 16
NORM_EPS = 1e-6
NEG_INF = -1e30
PROJ_WIDTH = 2 * SSM_WIDTH + 4 * ATT_WIDTH + 2 * D_MODEL
F32 = jnp.float32

kernel_name = 'hybrid_s5_moba_gated_step'


def rmsnorm(x, g):
    xf = x.astype(F32)
    xf = xf * lax.rsqrt(jnp.mean(xf * xf, axis=-1, keepdims=True) + NORM_EPS)
    return (xf * g.astype(F32)).astype(x.dtype)


def alibi_slopes():
    return 2.0 ** (-8.0 * jnp.arange(1, N_HEADS + 1, dtype=F32) / N_HEADS)


def s5_discretize(lam_re, lam_im, log_dt, b_re, b_im):
    dt = jnp.exp(log_dt.astype(F32))[:, None]
    lr = lam_re.astype(F32)
    li = lam_im.astype(F32)
    mag = jnp.exp(lr * dt)
    a_re = mag * jnp.cos(li * dt)
    a_im = mag * jnp.sin(li * dt)
    den = lr * lr + li * li
    coef_re = ((a_re - 1.0) * lr + a_im * li) / den
    coef_im = (a_im * lr - (a_re - 1.0) * li) / den
    cr = coef_re[..., None]
    ci = coef_im[..., None]
    br = b_re.astype(F32)
    bi = b_im.astype(F32)
    bbar_re = cr * br - ci * bi
    bbar_im = cr * bi + ci * br
    return a_re, a_im, bbar_re, bbar_im


def _cplx_combine(e_i, e_j):
    ar_i, ai_i, br_i, bi_i = e_i
    ar_j, ai_j, br_j, bi_j = e_j
    return (ar_j * ar_i - ai_j * ai_i,
            ar_j * ai_i + ai_j * ar_i,
            ar_j * br_i - ai_j * bi_i + br_j,
            ar_j * bi_i + ai_j * br_i + bi_j)


def s5_mix(u, disc, c_re, c_im, d_skip, h0_re=None, h0_im=None):
    a_re, a_im, bbar_re, bbar_im = disc
    nb, L, _ = u.shape
    uf = u.astype(F32).reshape(nb, L, SSM_GROUPS, SSM_GROUP)
    bu_re = jnp.einsum('blgh,gph->blgp', uf, bbar_re)
    bu_im = jnp.einsum('blgh,gph->blgp', uf, bbar_im)
    shp = (1, L, SSM_GROUPS, SSM_STATE)
    acum_re, acum_im, h_re, h_im = lax.associative_scan(
        _cplx_combine,
        (jnp.broadcast_to(a_re, shp), jnp.broadcast_to(a_im, shp), bu_re, bu_im),
        axis=1)
    if h0_re is not None:
        r0 = h0_re.astype(F32)[:, None]
        i0 = h0_im.astype(F32)[:, None]
        h_re, h_im = (h_re + acum_re * r0 - acum_im * i0,
                      h_im + acum_re * i0 + acum_im * r0)
    y = (jnp.einsum('blgp,ghp->blgh', h_re, c_re.astype(F32))
         - jnp.einsum('blgp,ghp->blgh', h_im, c_im.astype(F32)))
    y = y.reshape(nb, L, SSM_WIDTH) + d_skip.astype(F32) * u.astype(F32)
    return y.astype(u.dtype), h_re[:, -1], h_im[:, -1]


def in_proj(x, g, w_in):
    h = rmsnorm(x, g)
    p = h @ w_in
    sizes = (SSM_WIDTH, SSM_WIDTH, ATT_WIDTH, ATT_WIDTH, ATT_WIDTH, ATT_WIDTH, D_MODEL, D_MODEL)
    cuts = [int(c) for c in np.cumsum(sizes)[:-1]]
    return jnp.split(p, cuts, axis=-1)


def split_heads(t):
    nb, L, _ = t.shape
    return t.reshape(nb, L, N_HEADS, HEAD_DIM).transpose(0, 2, 1, 3)


def moba_select(q, block_mean, q_block, k_sel):
    s = jnp.einsum('...hqd,...hnd->...hqn', q.astype(F32), block_mean.astype(F32))
    nfp = block_mean.shape[-2]
    past = jnp.arange(nfp)[None, :] < q_block[:, None]
    s = jnp.where(past, s, NEG_INF)
    _, idx = lax.top_k(s, k_sel)
    ok = idx < q_block[:, None]
    return idx, ok


def moba_attend(q, t_pos, sel, own_k, own_v, own_pos, slopes):
    qf = q.astype(F32) * (HEAD_DIM ** -0.5)
    slope = slopes[:, None, None]
    d_own = (t_pos[:, None] - own_pos[None, :]).astype(F32)
    l_own = jnp.einsum('...hqd,...hkd->...hqk', qf, own_k.astype(F32)) - slope * d_own
    l_own = jnp.where(d_own >= 0, l_own, NEG_INF)
    if sel is None:
        p = jax.nn.softmax(l_own, axis=-1)
        out = jnp.einsum('...hqk,...hkd->...hqd', p, own_v.astype(F32))
        return out.astype(q.dtype)
    sel_k, sel_v, idx, ok = sel
    key_pos = idx[..., None] * MOBA_BLOCK + jnp.arange(MOBA_BLOCK)
    d_sel = (t_pos[:, None, None] - key_pos).astype(F32)
    l_sel = jnp.einsum('...hqd,...hqkjd->...hqkj', qf, sel_k.astype(F32)) - slope[..., None] * d_sel
    l_sel = jnp.where(ok[..., None], l_sel, NEG_INF)
    n_sel = l_sel.shape[-2] * l_sel.shape[-1]
    l_all = jnp.concatenate([l_sel.reshape(l_sel.shape[:-2] + (n_sel,)), l_own], axis=-1)
    p = jax.nn.softmax(l_all, axis=-1)
    p_sel = p[..., :n_sel].reshape(l_sel.shape)
    p_own = p[..., n_sel:]
    out = (jnp.einsum('...hqkj,...hqkjd->...hqd', p_sel, sel_v.astype(F32))
           + jnp.einsum('...hqk,...hkd->...hqd', p_own, own_v.astype(F32)))
    return out.astype(q.dtype)


def moba_prompt(q, k, v, slopes):
    nb, H, L, Dh = q.shape
    nfp = (L - 1) // MOBA_BLOCK
    n_blocks = -(-L // MOBA_BLOCK)
    pad = n_blocks * MOBA_BLOCK - L
    k_pad = jnp.pad(k, ((0, 0), (0, 0), (0, pad), (0, 0)))
    v_pad = jnp.pad(v, ((0, 0), (0, 0), (0, pad), (0, 0)))
    k_sel = min(MOBA_TOPK, nfp)
    if k_sel > 0:
        block_mean = k[:, :, :nfp * MOBA_BLOCK].astype(F32).reshape(nb, H, nfp, MOBA_BLOCK, Dh).mean(axis=3)
        kb = k_pad.reshape(nb, H, n_blocks, MOBA_BLOCK, Dh)
        vb = v_pad.reshape(nb, H, n_blocks, MOBA_BLOCK, Dh)
    bi = jnp.arange(nb)[:, None, None, None]
    hi = jnp.arange(H)[None, :, None, None]

    def chunk(c):
        start = c * Q_CHUNK
        t_pos = start + jnp.arange(Q_CHUNK)
        qb = start // MOBA_BLOCK
        q_c = lax.dynamic_slice_in_dim(q, start, Q_CHUNK, axis=2)
        own_start = qb * MOBA_BLOCK
        own_k = lax.dynamic_slice_in_dim(k_pad, own_start, MOBA_BLOCK, axis=2)
        own_v = lax.dynamic_slice_in_dim(v_pad, own_start, MOBA_BLOCK, axis=2)
        own_pos = own_start + jnp.arange(MOBA_BLOCK)
        sel = None
        if k_sel > 0:
            idx, ok = moba_select(q_c, block_mean, jnp.full((Q_CHUNK,), qb, jnp.int32), k_sel)
            sel = (kb[bi, hi, idx], vb[bi, hi, idx], idx, ok)
        return moba_attend(q_c, t_pos, sel, own_k, own_v, own_pos, slopes)

    out = lax.map(chunk, jnp.arange(L // Q_CHUNK))
    return out.transpose(1, 0, 3, 2, 4).reshape(nb, L, H * Dh)


def moba_sample(q, k_new, v_new, cache_k, cache_v, layer, page_table, slopes):
    ndb, H, T, Dh = q.shape
    ppb = MOBA_BLOCK // PAGE_SIZE
    nfp = PAST_LEN // MOBA_BLOCK
    k_sel = min(MOBA_TOPK, nfp)
    own_past = PAST_LEN % MOBA_BLOCK
    own_start = PAST_LEN - own_past
    t_pos = PAST_LEN + jnp.arange(T)
    q_block = jnp.full((T,), nfp, jnp.int32)
    own_pos = own_start + jnp.arange(own_past + T)
    hi = jnp.arange(H)[:, None, None, None]

    def one_seq(args):
        q_s, k_s, v_s, pt = args
        own_k, own_v = k_s, v_s
        if own_past > 0:
            op = pt[own_start // PAGE_SIZE: PAST_LEN // PAGE_SIZE]
            rk = cache_k[layer, op].transpose(1, 0, 2, 3).reshape(H, own_past, Dh)
            rv = cache_v[layer, op].transpose(1, 0, 2, 3).reshape(H, own_past, Dh)
            own_k = jnp.concatenate([rk.astype(k_s.dtype), k_s], axis=1)
            own_v = jnp.concatenate([rv.astype(v_s.dtype), v_s], axis=1)
        sel = None
        if k_sel > 0:
            past_k = cache_k[layer, pt[:nfp * ppb]]
            bm = past_k.astype(F32).reshape(nfp, ppb, H, PAGE_SIZE, Dh).mean(axis=(1, 3)).transpose(1, 0, 2)
            idx, ok = moba_select(q_s, bm, q_block, k_sel)
            phys = pt[idx[..., None] * ppb + jnp.arange(ppb)]
            sk = cache_k[layer, phys, hi].reshape(H, T, k_sel, MOBA_BLOCK, Dh)
            sv = cache_v[layer, phys, hi].reshape(H, T, k_sel, MOBA_BLOCK, Dh)
            sel = (sk, sv, idx, ok)
        return moba_attend(q_s, t_pos, sel, own_k, own_v, own_pos, slopes)

    out = lax.map(one_seq, (q, k_new, v_new, page_table))
    return out.transpose(0, 2, 1, 3).reshape(ndb, T, H * Dh)


def merge_branches(x, y_ssm, z_ssm, y_att, z_att, g_ssm, g_att, w_glu, w_br_ssm, w_br_att, w_out):
    s = jax.nn.gelu(y_ssm)
    s = s * jax.nn.sigmoid(s @ w_glu)
    s = s * jax.nn.silu(z_ssm)
    a = y_att * jax.nn.silu(z_att)
    mixed = jax.nn.sigmoid(g_ssm) * (s @ w_br_ssm) + jax.nn.sigmoid(g_att) * (a @ w_br_att)
    return x + mixed @ w_out


def setup_inputs(seed: int = 0) -> dict:
    key = jax.random.key(seed)
    ks = jax.random.split(key, 24)
    nrm = jax.random.normal
    n_pages = PAST_LEN // PAGE_SIZE
    n_pool = (DEC_BATCH * n_pages * 5) // 4
    x_prompt = nrm(ks[0], (BATCH, SEQ, D_MODEL), F32)
    x_sample = nrm(ks[1], (DEC_BATCH, DEC_SEQ, D_MODEL), F32)
    cache_k = nrm(ks[2], (DEPTH, n_pool, N_HEADS, PAGE_SIZE, HEAD_DIM), F32)
    cache_v = nrm(ks[3], (DEPTH, n_pool, N_HEADS, PAGE_SIZE, HEAD_DIM), F32)
    perm = jax.random.permutation(ks[4], n_pool)
    page_table = perm[:DEC_BATCH * n_pages].reshape(DEC_BATCH, n_pages).astype(jnp.int32)
    state_ssm_re = 0.5 * nrm(ks[5], (DEPTH, DEC_BATCH, SSM_GROUPS, SSM_STATE), F32)
    state_ssm_im = 0.5 * nrm(ks[6], (DEPTH, DEC_BATCH, SSM_GROUPS, SSM_STATE), F32)
    norm_g = 1.0 + 0.02 * nrm(ks[7], (DEPTH, D_MODEL), F32)
    w_in = nrm(ks[8], (DEPTH, D_MODEL, PROJ_WIDTH), F32) * D_MODEL ** -0.5
    n_idx = jnp.arange(SSM_STATE, dtype=F32)
    ssm_lambda_re = -0.5 + 0.01 * nrm(ks[9], (DEPTH, SSM_GROUPS, SSM_STATE), F32)
    ssm_lambda_im = jnp.pi * n_idx + 0.01 * nrm(ks[10], (DEPTH, SSM_GROUPS, SSM_STATE), F32)
    ssm_log_dt = jax.random.uniform(ks[11], (DEPTH, SSM_GROUPS), F32, math.log(DT_MIN), math.log(DT_MAX))
    ssm_b_re = nrm(ks[12], (DEPTH, SSM_GROUPS, SSM_STATE, SSM_GROUP), F32) * (2 * SSM_GROUP) ** -0.5
    ssm_b_im = nrm(ks[13], (DEPTH, SSM_GROUPS, SSM_STATE, SSM_GROUP), F32) * (2 * SSM_GROUP) ** -0.5
    ssm_c_re = nrm(ks[14], (DEPTH, SSM_GROUPS, SSM_GROUP, SSM_STATE), F32) * (2 * SSM_STATE) ** -0.5
    ssm_c_im = nrm(ks[15], (DEPTH, SSM_GROUPS, SSM_GROUP, SSM_STATE), F32) * (2 * SSM_STATE) ** -0.5
    ssm_d = nrm(ks[16], (DEPTH, SSM_WIDTH), F32)
    w_glu = nrm(ks[17], (DEPTH, SSM_WIDTH, SSM_WIDTH), F32) * SSM_WIDTH ** -0.5
    w_branch_ssm = nrm(ks[18], (DEPTH, SSM_WIDTH, D_MODEL), F32) * SSM_WIDTH ** -0.5
    w_branch_att = nrm(ks[19], (DEPTH, ATT_WIDTH, D_MODEL), F32) * ATT_WIDTH ** -0.5
    w_out = nrm(ks[20], (DEPTH, D_MODEL, D_MODEL), F32) * D_MODEL ** -0.5
    final_norm_g = 1.0 + 0.02 * nrm(ks[21], (D_MODEL,), F32)
    return {'x_prompt': x_prompt, 'x_sample': x_sample, 'cache_k': cache_k, 'cache_v': cache_v,
            'page_table': page_table, 'state_ssm_re': state_ssm_re, 'state_ssm_im': state_ssm_im,
            'norm_g': norm_g, 'w_in': w_in, 'ssm_lambda_re': ssm_lambda_re, 'ssm_lambda_im': ssm_lambda_im,
            'ssm_log_dt': ssm_log_dt, 'ssm_b_re': ssm_b_re, 'ssm_b_im': ssm_b_im, 'ssm_c_re': ssm_c_re,
            'ssm_c_im': ssm_c_im, 'ssm_d': ssm_d, 'w_glu': w_glu, 'w_branch_ssm': w_branch_ssm,
            'w_branch_att': w_branch_att, 'w_out': w_out, 'final_norm_g': final_norm_g}


def reference(x_prompt, x_sample, cache_k, cache_v, page_table, state_ssm_re, state_ssm_im,
              norm_g, w_in, ssm_lambda_re, ssm_lambda_im, ssm_log_dt, ssm_b_re, ssm_b_im,
              ssm_c_re, ssm_c_im, ssm_d, w_glu, w_branch_ssm, w_branch_att, w_out, final_norm_g):
    slopes = alibi_slopes()
    hp, hs = x_prompt, x_sample
    kp_l, vp_l, rp_l, ip_l = [], [], [], []
    ks_l, vs_l, rs_l, is_l = [], [], [], []
    for l in range(DEPTH):
        disc = s5_discretize(ssm_lambda_re[l], ssm_lambda_im[l], ssm_log_dt[l], ssm_b_re[l], ssm_b_im[l])
        u_a, z_a, q, k, v, z_b, g_a, g_b = in_proj(hp, norm_g[l], w_in[l])
        y_a, h_re, h_im = s5_mix(u_a, disc, ssm_c_re[l], ssm_c_im[l], ssm_d[l])
        kh, vh = split_heads(k), split_heads(v)
        y_b = moba_prompt(split_heads(q), kh, vh, slopes)
        hp = merge_branches(hp, y_a, z_a, y_b, z_b, g_a, g_b, w_glu[l], w_branch_ssm[l], w_branch_att[l], w_out[l])
        kp_l.append(kh)
        vp_l.append(vh)
        rp_l.append(h_re)
        ip_l.append(h_im)
        u_a, z_a, q, k, v, z_b, g_a, g_b = in_proj(hs, norm_g[l], w_in[l])
        y_a, h_re, h_im = s5_mix(u_a, disc, ssm_c_re[l], ssm_c_im[l], ssm_d[l], state_ssm_re[l], state_ssm_im[l])
        kh, vh = split_heads(k), split_heads(v)
        y_b = moba_sample(split_heads(q), kh, vh, cache_k, cache_v, l, page_table, slopes)
        hs = merge_branches(hs, y_a, z_a, y_b, z_b, g_a, g_b, w_glu[l], w_branch_ssm[l], w_branch_att[l], w_out[l])
        ks_l.append(kh)
        vs_l.append(vh)
        rs_l.append(h_re)
        is_l.append(h_im)
    y_prompt = rmsnorm(hp, final_norm_g)
    y_sample = rmsnorm(hs, final_norm_g)
    new_k_prompt = jnp.stack(kp_l)
    new_v_prompt = jnp.stack(vp_l)
    new_ssm_re_prompt = jnp.stack(rp_l)
    new_ssm_im_prompt = jnp.stack(ip_l)
    new_k_sample = jnp.stack(ks_l)
    new_v_sample = jnp.stack(vs_l)
    new_ssm_re_sample = jnp.stack(rs_l)
    new_ssm_im_sample = jnp.stack(is_l)
    return (y_prompt, y_sample, new_k_prompt, new_v_prompt, new_ssm_re_prompt, new_ssm_im_prompt,
            new_k_sample, new_v_sample, new_ssm_re_sample, new_ssm_im_sample)
```

```python
import functools
import math

import jax
import jax.numpy as jnp
from jax import lax
from jax.experimental import pallas as pl
from jax.experimental.pallas import tpu as pltpu

F32 = jnp.float32
BF16 = jnp.bfloat16

N_HEADS = 8
HEAD_DIM = 128
SSM_GROUP = 16
SSM_STATE = 64
MOBA_BLOCK = 256
MOBA_TOPK = 3
PAGE_SIZE = 128
NORM_EPS = 1e-6
NEG_INF = -1e30
N_SEG = 8

VMEM_LIMIT = 56 * 1024 * 1024
HIGHEST = lax.Precision.HIGHEST


def _cparams(sem):
    return pltpu.CompilerParams(dimension_semantics=sem, vmem_limit_bytes=VMEM_LIMIT)


def _sigmoid(x):
    return 1.0 / (1.0 + jnp.exp(-x))


def _silu(x):
    return x * _sigmoid(x)


def _gelu_tanh(x):
    c = math.sqrt(2.0 / math.pi)
    return 0.5 * x * (1.0 + jnp.tanh(c * (x + 0.044715 * (x * x * x))))


def _dot_nt(a, b, **kw):
    return lax.dot_general(a, b, (((1,), (1,)), ((), ())), preferred_element_type=F32, **kw)


def _in_proj_kernel(x_ref, g_ref, w_ref, *out_refs, width, with_attn_copies):
    if with_attn_copies:
        u_ref, za_ref, q_ref, k_ref, v_ref, zb_ref, ga_ref, gb_ref, kb_ref, vb_ref, km_ref = out_refs
    else:
        u_ref, za_ref, q_ref, k_ref, v_ref, zb_ref, ga_ref, gb_ref = out_refs
    x = x_ref[0]
    ms = jnp.mean(x * x, axis=-1, keepdims=True)
    xn = (x * lax.rsqrt(ms + NORM_EPS) * g_ref[...]).astype(BF16)

    def seg(s):
        return jnp.dot(xn, w_ref[:, s * width:(s + 1) * width], preferred_element_type=F32)

    u_ref[0] = seg(0).astype(u_ref.dtype)
    za_ref[0] = seg(1).astype(za_ref.dtype)
    pq = seg(2)
    for h in range(N_HEADS):
        q_ref[0, h] = pq[:, h * HEAD_DIM:(h + 1) * HEAD_DIM]
    pk = seg(3)
    for h in range(N_HEADS):
        k_ref[0, h] = pk[:, h * HEAD_DIM:(h + 1) * HEAD_DIM]
    pv = seg(4)
    for h in range(N_HEADS):
        v_ref[0, h] = pv[:, h * HEAD_DIM:(h + 1) * HEAD_DIM]
    if with_attn_copies:
        for h in range(N_HEADS):
            kb_ref[0, h] = pk[:, h * HEAD_DIM:(h + 1) * HEAD_DIM].astype(BF16)
            vb_ref[0, h] = pv[:, h * HEAD_DIM:(h + 1) * HEAD_DIM].astype(BF16)
        tm = pk.shape[0]
        for blk in range(tm // MOBA_BLOCK):
            km_ref[0, 0, blk:blk + 1, :] = jnp.mean(
                pk[blk * MOBA_BLOCK:(blk + 1) * MOBA_BLOCK], axis=0, keepdims=True)
    zb_ref[0] = seg(5).astype(zb_ref.dtype)
    ga_ref[0] = seg(6).astype(ga_ref.dtype)
    gb_ref[0] = seg(7).astype(gb_ref.dtype)


def _in_proj(x, g, w_bf16, *, tm, with_attn_copies):
    nb, L, D = x.shape
    width = w_bf16.shape[1] // N_SEG
    assert width == N_HEADS * HEAD_DIM and L % tm == 0
    flat = lambda dt: jax.ShapeDtypeStruct((nb, L, width), dt)
    heads = lambda dt: jax.ShapeDtypeStruct((nb, N_HEADS, L, HEAD_DIM), dt)
    flat_spec = pl.BlockSpec((1, tm, width), lambda b, i: (b, i, 0))
    head_spec = pl.BlockSpec((1, N_HEADS, tm, HEAD_DIM), lambda b, i: (b, 0, i, 0))
    out_shape = [flat(BF16), flat(BF16), heads(F32), heads(F32), heads(F32), flat(BF16), flat(BF16), flat(BF16)]
    out_specs = [flat_spec, flat_spec, head_spec, head_spec, head_spec, flat_spec, flat_spec, flat_spec]
    if with_attn_copies:
        assert tm % MOBA_BLOCK == 0
        nblk = tm // MOBA_BLOCK
        out_shape += [heads(BF16), heads(BF16), jax.ShapeDtypeStruct((nb, L // tm, nblk, width), F32)]
        out_specs += [head_spec, head_spec, pl.BlockSpec((1, 1, nblk, width), lambda b, i: (b, i, 0, 0))]
    return pl.pallas_call(
        functools.partial(_in_proj_kernel, width=width, with_attn_copies=with_attn_copies),
        grid=(nb, L // tm),
        in_specs=[
            pl.BlockSpec((1, tm, D), lambda b, i: (b, i, 0)),
            pl.BlockSpec((1, D), lambda b, i: (0, 0)),
            pl.BlockSpec(w_bf16.shape, lambda b, i: (0, 0), pipeline_mode=pl.Buffered(1)),
        ],
        out_specs=out_specs,
        out_shape=out_shape,
        compiler_params=_cparams(("parallel", "parallel")),
        name="in_proj",
    )(x, g.reshape(1, D), w_bf16)


def _s5_weights(lam_re, lam_im, log_dt, b_re, b_im, c_re, c_im, d_skip, T):
    G, P = lam_re.shape
    H = SSM_GROUP
    dt = jnp.exp(log_dt.astype(F32))[:, None]
    lr = lam_re.astype(F32)
    li = lam_im.astype(F32)
    mag = jnp.exp(lr * dt)
    a_re = mag * jnp.cos(li * dt)
    a_im = mag * jnp.sin(li * dt)
    den = lr * lr + li * li
    coef_re = ((a_re - 1.0) * lr + a_im * li) / den
    coef_im = (a_im * lr - (a_re - 1.0) * li) / den
    br = b_re.astype(F32)
    bi = b_im.astype(F32)
    bbar_re = coef_re[..., None] * br - coef_im[..., None] * bi
    bbar_im = coef_re[..., None] * bi + coef_im[..., None] * br
    j = jnp.arange(T + 1, dtype=F32)[:, None, None]
    pm = jnp.exp(lr * dt * j)
    pr = pm * jnp.cos(li * dt * j)
    pi = pm * jnp.sin(li * dt * j)
    apr = pr[..., None] * bbar_re - pi[..., None] * bbar_im
    api = pr[..., None] * bbar_im + pi[..., None] * bbar_re
    cr = c_re.astype(F32)
    ci = c_im.astype(F32)
    kj = (jnp.einsum('ghp,jgpk->jghk', cr, apr[:T], precision=HIGHEST)
          - jnp.einsum('ghp,jgpk->jghk', ci, api[:T], precision=HIGHEST))
    lag = jnp.arange(T)[None, :] - jnp.arange(T)[:, None]
    kt = jnp.where((lag >= 0)[:, :, None, None, None], kj[jnp.clip(lag, 0, T - 1)], 0.0)
    toep = kt.transpose(2, 0, 4, 1, 3)
    eye_t = jnp.eye(T, dtype=F32)
    eye_h = jnp.eye(H, dtype=F32)
    dmat = d_skip.astype(F32).reshape(G, H)
    toep = toep + (eye_t[None, :, None, :, None] * eye_h[None, None, :, None, :]
                   * dmat[:, None, None, None, :])
    toep = toep.reshape(G, T * H, T * H)
    wr = apr[:T][::-1].transpose(1, 0, 3, 2)
    wi = api[:T][::-1].transpose(1, 0, 3, 2)
    TH = T * H
    wr = wr.reshape(G // 2, 2, TH, P)
    wi = wi.reshape(G // 2, 2, TH, P)
    z = jnp.zeros_like(wr[:, 0])
    wst = jnp.concatenate([
        jnp.concatenate([wr[:, 0], z, wi[:, 0], z], axis=-1),
        jnp.concatenate([z, wr[:, 1], z, wi[:, 1]], axis=-1)], axis=1)
    p1r = pr[1:]
    p1i = pi[1:]
    car = cr[None] * p1r[:, :, None, :] - ci[None] * p1i[:, :, None, :]
    cai = cr[None] * p1i[:, :, None, :] + ci[None] * p1r[:, :, None, :]
    from_re = car.transpose(1, 3, 0, 2).reshape(G // 2, 2, P, TH)
    from_im = (-cai).transpose(1, 3, 0, 2).reshape(G // 2, 2, P, TH)
    zc = jnp.zeros_like(from_re[:, 0])
    cst = jnp.concatenate([
        jnp.concatenate([from_re[:, 0], zc], axis=-1),
        jnp.concatenate([zc, from_re[:, 1]], axis=-1),
        jnp.concatenate([from_im[:, 0], zc], axis=-1),
        jnp.concatenate([zc, from_im[:, 1]], axis=-1)], axis=1)
    at = jnp.stack([pr[T].reshape(G // 2, 2 * P), pi[T].reshape(G // 2, 2 * P)], axis=1)
    return toep.astype(BF16), wst.astype(BF16), cst.astype(BF16), at


def _s5_kernel(x_ref, toep_ref, wst_ref, cst_ref, at_ref, h0r_ref, h0i_ref,
               y_ref, hr_ref, hi_ref, s_scr, hst_scr, *, n_chunks, rb, th):
    x = x_ref[0]
    sw = 2 * SSM_STATE
    s_scr[...] = jnp.dot(x, wst_ref[0], preferred_element_type=F32)
    ar = jnp.broadcast_to(at_ref[0, 0:1, :], (rb, sw))
    ai = jnp.broadcast_to(at_ref[0, 1:2, :], (rb, sw))

    def chunk_step(c, carry):
        hr, hi = carry
        r0 = pl.multiple_of(c * rb, rb)
        hst_scr[pl.ds(r0, rb), 0:sw] = hr
        hst_scr[pl.ds(r0, rb), sw:2 * sw] = hi
        sr = s_scr[pl.ds(r0, rb), 0:sw]
        si = s_scr[pl.ds(r0, rb), sw:2 * sw]
        return ar * hr - ai * hi + sr, ar * hi + ai * hr + si

    hr, hi = lax.fori_loop(0, n_chunks, chunk_step, (h0r_ref[0], h0i_ref[0]))
    hr_ref[0] = hr
    hi_ref[0] = hi
    ys = jnp.dot(hst_scr[...].astype(BF16), cst_ref[0], preferred_element_type=F32)
    y0 = jnp.dot(x[:, :th], toep_ref[0], preferred_element_type=F32)
    y1 = jnp.dot(x[:, th:], toep_ref[1], preferred_element_type=F32)
    y_ref[0, :, :th] = (y0 + ys[:, :th]).astype(y_ref.dtype)
    y_ref[0, :, th:] = (y1 + ys[:, th:]).astype(y_ref.dtype)


def _s5_mix(u, weights, h0_re, h0_im, T):
    toep, wst, cst, at = weights
    nb, L, W = u.shape
    G = W // SSM_GROUP
    P = SSM_STATE
    np_ = G // 2
    C = L // T
    th = T * SSM_GROUP
    x = u.reshape(nb, C, T, np_, 2, SSM_GROUP).transpose(3, 1, 0, 4, 2, 5).reshape(np_, C * nb, 2 * th)
    h0r = h0_re.reshape(nb, np_, 2 * P).transpose(1, 0, 2)
    h0i = h0_im.reshape(nb, np_, 2 * P).transpose(1, 0, 2)
    rows = C * nb
    y, hr, hi = pl.pallas_call(
        functools.partial(_s5_kernel, n_chunks=C, rb=nb, th=th),
        grid=(np_,),
        in_specs=[
            pl.BlockSpec((1, rows, 2 * th), lambda p: (p, 0, 0)),
            pl.BlockSpec((2, th, th), lambda p: (p, 0, 0)),
            pl.BlockSpec((1, 2 * th, 4 * P), lambda p: (p, 0, 0)),
            pl.BlockSpec((1, 4 * P, 2 * th), lambda p: (p, 0, 0)),
            pl.BlockSpec((1, 2, 2 * P), lambda p: (p, 0, 0)),
            pl.BlockSpec((1, nb, 2 * P), lambda p: (p, 0, 0)),
            pl.BlockSpec((1, nb, 2 * P), lambda p: (p, 0, 0)),
        ],
        out_specs=[
            pl.BlockSpec((1, rows, 2 * th), lambda p: (p, 0, 0)),
            pl.BlockSpec((1, nb, 2 * P), lambda p: (p, 0, 0)),
            pl.BlockSpec((1, nb, 2 * P), lambda p: (p, 0, 0)),
        ],
        out_shape=[
            jax.ShapeDtypeStruct((np_, rows, 2 * th), BF16),
            jax.ShapeDtypeStruct((np_, nb, 2 * P), F32),
            jax.ShapeDtypeStruct((np_, nb, 2 * P), F32),
        ],
        scratch_shapes=[pltpu.VMEM((rows, 4 * P), F32), pltpu.VMEM((rows, 4 * P), F32)],
        compiler_params=_cparams(("parallel",)),
        name="s5_mix",
    )(x, toep, wst, cst, at, h0r, h0i)
    y = y.reshape(np_, C, nb, 2, T, SSM_GROUP).transpose(2, 1, 4, 0, 3, 5).reshape(nb, L, W)
    hr = hr.transpose(1, 0, 2).reshape(nb, G, P)
    hi = hi.transpose(1, 0, 2).reshape(nb, G, P)
    return y, hr, hi


def _topk_rank(s, n_cand, valid_fn):
    jidx = lax.broadcasted_iota(jnp.int32, s.shape, 1)
    rank = jnp.zeros(s.shape, jnp.int32)
    for jp in range(n_cand):
        col = s[:, jp:jp + 1]
        beats = (col > s) | ((col == s) & (jp < jidx))
        if valid_fn is not None:
            beats = beats & valid_fn(jp)
        rank = rank + beats.astype(jnp.int32)
    return rank, jidx


def _moba_prompt_kernel(q_ref, km_ref, kb_ref, vb_ref, d_ref, sl_ref, o_ref, *, n_blocks):
    i = pl.program_id(2)
    bs = MOBA_BLOCK
    q = q_ref[0, 0]
    means = km_ref[0]
    s = _dot_nt(q, means, precision=HIGHEST)
    rank, jidx = _topk_rank(s, n_blocks - 1, lambda jp: jp < i)
    sel = jnp.where((rank < MOBA_TOPK) & (jidx < i), 1.0, 0.0)
    qs = (q * (HEAD_DIM ** -0.5)).astype(BF16)
    dmat = d_ref[0]
    slope_blk = sl_ref[0][:, :1] * float(bs)
    rows = lax.broadcasted_iota(jnp.int32, (bs, bs), 0)
    cols = lax.broadcasted_iota(jnp.int32, (bs, bs), 1)

    def attend(j, allowed, carry):
        m, l, acc = carry
        r0 = pl.multiple_of(j * bs, bs)
        kj = kb_ref[0, 0, pl.ds(r0, bs), :]
        vj = vb_ref[0, 0, pl.ds(r0, bs), :]
        lg = _dot_nt(qs, kj) - dmat - slope_blk * (i - j).astype(F32)
        lg = jnp.where(allowed, lg, NEG_INF)
        m_new = jnp.maximum(m, jnp.max(lg, axis=1, keepdims=True))
        alpha = jnp.exp(m - m_new)
        p = jnp.exp(lg - m_new)
        l = alpha * l + jnp.sum(p, axis=1, keepdims=True)
        acc = alpha * acc + jnp.dot(p.astype(BF16), vj, preferred_element_type=F32)
        return m_new, l, acc

    init = (jnp.full((bs, 1), NEG_INF, F32), jnp.zeros((bs, 1), F32), jnp.zeros((bs, HEAD_DIM), F32))
    carry = attend(i, cols <= rows, init)

    def past(j, carry):
        selj = jnp.sum(jnp.where(jidx == j, sel, 0.0), axis=1, keepdims=True) > 0.5
        return attend(j, jnp.broadcast_to(selj, (bs, bs)), carry)

    m, l, acc = lax.fori_loop(0, i, past, carry)
    o_ref[0] = (acc / l).astype(o_ref.dtype)


def _moba_prompt(q, kmeans, kb, vb, slopes):
    nb, H, L, Dh = q.shape
    bs = MOBA_BLOCK
    n_blocks = L // bs
    r = jnp.arange(bs, dtype=F32)
    dmat = slopes[:, None, None] * (r[:, None] - r[None, :])[None]
    sl = jnp.broadcast_to(slopes[:, None, None], (H, 1, 128))
    return pl.pallas_call(
        functools.partial(_moba_prompt_kernel, n_blocks=n_blocks),
        grid=(nb, H, n_blocks),
        in_specs=[
            pl.BlockSpec((1, 1, bs, Dh), lambda b, h, i: (b, h, i, 0)),
            pl.BlockSpec((1, n_blocks, Dh), lambda b, h, i: (b, 0, h)),
            pl.BlockSpec((1, 1, L, Dh), lambda b, h, i: (b, h, 0, 0)),
            pl.BlockSpec((1, 1, L, Dh), lambda b, h, i: (b, h, 0, 0)),
            pl.BlockSpec((1, bs, bs), lambda b, h, i: (h, 0, 0)),
            pl.BlockSpec((1, 1, 128), lambda b, h, i: (h, 0, 0)),
        ],
        out_specs=pl.BlockSpec((1, bs, Dh), lambda b, h, i: (b, i, h)),
        out_shape=jax.ShapeDtypeStruct((nb, L, H * Dh), BF16),
        compiler_params=_cparams(("parallel", "parallel", "arbitrary")),
        name="moba_prompt",
    )(q, kmeans, kb, vb, dmat, sl)


PAGES_PER_STEP = 8
BLOCKS_PER_STEP = PAGES_PER_STEP * PAGE_SIZE // MOBA_BLOCK


def _page_specs(n_pages):
    def spec(e):
        return pl.BlockSpec((1, 1, N_HEADS, PAGE_SIZE, HEAD_DIM),
                            lambda s, g, pt: (0, pt[s * n_pages + g * PAGES_PER_STEP + e], 0, 0, 0))
    return [spec(e) for e in range(PAGES_PER_STEP)]


def _block_from_pages(page_refs, e2, h):
    ppb = MOBA_BLOCK // PAGE_SIZE
    return jnp.concatenate([page_refs[ppb * e2 + t][0, 0, h] for t in range(ppb)], axis=0)


def _moba_sample_probs_kernel(pt_ref, q_ref, kn_ref, sl_ref, *refs, n_past_blocks, past_len):
    k_pages = refs[:PAGES_PER_STEP]
    p_ref, po_ref, lg_scr, sum_scr = refs[PAGES_PER_STEP:]
    g = pl.program_id(1)
    bs = MOBA_BLOCK
    T = q_ref.shape[2]
    scale = HEAD_DIM ** -0.5
    for e2 in range(BLOCKS_PER_STEP):
        blk = g * BLOCKS_PER_STEP + e2
        for h in range(N_HEADS):
            kblk = _block_from_pages(k_pages, e2, h)
            sum_scr[h, pl.ds(blk, 1), :] = jnp.sum(kblk, axis=0, keepdims=True)
            qs = (q_ref[0, h] * scale).astype(BF16)
            lg_scr[h, blk] = _dot_nt(qs, kblk.astype(BF16))

    @pl.when(g == pl.num_programs(1) - 1)
    def _():
        trow = lax.broadcasted_iota(jnp.int32, (T, bs), 0)
        ccol = lax.broadcasted_iota(jnp.int32, (T, bs), 1)
        dist0 = (past_len + trow - ccol).astype(F32)
        pad = jnp.zeros((128 - T, HEAD_DIM), F32)
        trow_o = lax.broadcasted_iota(jnp.int32, (T, 128), 0)
        ccol_o = lax.broadcasted_iota(jnp.int32, (T, 128), 1)

        def head(h, _):
            qh = q_ref[0, h]
            means = sum_scr[h] * (1.0 / bs)
            sc = _dot_nt(qh, means, precision=HIGHEST)
            rank, _ = _topk_rank(sc, n_past_blocks, None)
            sel = rank < MOBA_TOPK
            slope = sl_ref[h][:, :1]
            qs = (qh * scale).astype(BF16)
            kn = jnp.concatenate([kn_ref[0, h], pad], axis=0).astype(BF16)
            l_own = _dot_nt(qs, kn) - slope * (trow_o - ccol_o).astype(F32)
            l_own = jnp.where(ccol_o <= trow_o, l_own, NEG_INF)
            m = jnp.max(l_own, axis=1, keepdims=True)
            ls = []
            for b in range(n_past_blocks):
                lb = lg_scr[h, b] - slope * (dist0 - float(b * bs))
                lb = jnp.where(sel[:, b:b + 1], lb, NEG_INF)
                ls.append(lb)
                m = jnp.maximum(m, jnp.max(lb, axis=1, keepdims=True))
            p_own = jnp.exp(l_own - m)
            den = jnp.sum(p_own, axis=1, keepdims=True)
            ps = []
            for lb in ls:
                pb = jnp.exp(lb - m)
                ps.append(pb)
                den = den + jnp.sum(pb, axis=1, keepdims=True)
            inv = 1.0 / den
            po_ref[0, h] = p_own * inv
            for b in range(n_past_blocks):
                p_ref[0, h, b] = ps[b] * inv
            return 0

        lax.fori_loop(0, N_HEADS, head, 0)


def _moba_sample_pv_kernel(pt_ref, p_ref, po_ref, vn_ref, *refs):
    v_pages = refs[:PAGES_PER_STEP]
    o_ref, acc_scr = refs[PAGES_PER_STEP:]
    g = pl.program_id(1)
    T = vn_ref.shape[2]

    @pl.when(g == 0)
    def _():
        pad = jnp.zeros((128 - T, HEAD_DIM), F32)
        for h in range(N_HEADS):
            vn = jnp.concatenate([vn_ref[0, h], pad], axis=0).astype(BF16)
            acc_scr[h] = jnp.dot(po_ref[0, h].astype(BF16), vn, preferred_element_type=F32)

    for e2 in range(BLOCKS_PER_STEP):
        for h in range(N_HEADS):
            vblk = _block_from_pages(v_pages, e2, h).astype(BF16)
            acc_scr[h] += jnp.dot(p_ref[0, h, e2].astype(BF16), vblk, preferred_element_type=F32)

    @pl.when(g == pl.num_programs(1) - 1)
    def _():
        for h in range(N_HEADS):
            o_ref[0, :, h * HEAD_DIM:(h + 1) * HEAD_DIM] = acc_scr[h].astype(o_ref.dtype)


def _moba_sample(q, k_new, v_new, cache_k, cache_v, page_table, slopes, past_len):
    ndb, H, T, Dh = q.shape
    n_pages = page_table.shape[1]
    assert past_len % MOBA_BLOCK == 0 and n_pages * PAGE_SIZE == past_len and n_pages % PAGES_PER_STEP == 0
    n_past_blocks = past_len // MOBA_BLOCK
    n_steps = n_pages // PAGES_PER_STEP
    bs = MOBA_BLOCK
    pt = page_table.reshape(-1).astype(jnp.int32)
    sl = jnp.broadcast_to(slopes[:, None, None], (H, 1, 128))
    qkv_spec = pl.BlockSpec((1, H, T, Dh), lambda s, g, pt: (s, 0, 0, 0))
    probs, p_own = pl.pallas_call(
        functools.partial(_moba_sample_probs_kernel, n_past_blocks=n_past_blocks, past_len=past_len),
        grid_spec=pltpu.PrefetchScalarGridSpec(
            num_scalar_prefetch=1,
            grid=(ndb, n_steps),
            in_specs=[qkv_spec, qkv_spec, pl.BlockSpec((H, 1, 128), lambda s, g, pt: (0, 0, 0))]
            + _page_specs(n_pages),
            out_specs=[
                pl.BlockSpec((1, H, n_past_blocks, T, bs), lambda s, g, pt: (s, 0, 0, 0, 0)),
                pl.BlockSpec((1, H, T, 128), lambda s, g, pt: (s, 0, 0, 0)),
            ],
            scratch_shapes=[pltpu.VMEM((H, n_past_blocks, T, bs), F32), pltpu.VMEM((H, n_past_blocks, Dh), F32)],
        ),
        out_shape=[
            jax.ShapeDtypeStruct((ndb, H, n_past_blocks, T, bs), F32),
            jax.ShapeDtypeStruct((ndb, H, T, 128), F32),
        ],
        compiler_params=_cparams(("parallel", "arbitrary")),
        name="moba_sample_probs",
    )(pt, q, k_new, sl, *([cache_k] * PAGES_PER_STEP))
    return pl.pallas_call(
        _moba_sample_pv_kernel,
        grid_spec=pltpu.PrefetchScalarGridSpec(
            num_scalar_prefetch=1,
            grid=(ndb, n_steps),
            in_specs=[
                pl.BlockSpec((1, H, BLOCKS_PER_STEP, T, bs), lambda s, g, pt: (s, 0, g, 0, 0)),
                pl.BlockSpec((1, H, T, 128), lambda s, g, pt: (s, 0, 0, 0)),
                qkv_spec,
            ] + _page_specs(n_pages),
            out_specs=pl.BlockSpec((1, T, H * Dh), lambda s, g, pt: (s, 0, 0)),
            scratch_shapes=[pltpu.VMEM((H, T, Dh), F32)],
        ),
        out_shape=jax.ShapeDtypeStruct((ndb, T, H * Dh), BF16),
        compiler_params=_cparams(("parallel", "arbitrary")),
        name="moba_sample_pv",
    )(pt, probs, p_own, v_new, *([cache_v] * PAGES_PER_STEP))


def _merge_kernel(x_ref, ya_ref, za_ref, yb_ref, zb_ref, ga_ref, gb_ref,
                  wglu_ref, wbs_ref, wba_ref, wout_ref, gf_ref, o_ref):
    f = lambda r: r[...].astype(F32)
    s = _gelu_tanh(f(ya_ref))
    s = s * _sigmoid(jnp.dot(s.astype(BF16), wglu_ref[...], preferred_element_type=F32))
    s = s * _silu(f(za_ref))
    a = f(yb_ref) * _silu(f(zb_ref))
    mixed = (_sigmoid(f(ga_ref)) * jnp.dot(s.astype(BF16), wbs_ref[...], preferred_element_type=F32)
             + _sigmoid(f(gb_ref)) * jnp.dot(a.astype(BF16), wba_ref[...], preferred_element_type=F32))
    o = x_ref[...] + jnp.dot(mixed.astype(BF16), wout_ref[...], preferred_element_type=F32)
    ms = jnp.mean(o * o, axis=-1, keepdims=True)
    o_ref[...] = o * lax.rsqrt(ms + NORM_EPS) * gf_ref[...]


def _merge(x, ya, za, yb, zb, ga, gb, wglu, wbs, wba, wout, gf, *, tm):
    n, D = x.shape
    assert n % tm == 0
    row = pl.BlockSpec((tm, D), lambda i: (i, 0))
    wspec = pl.BlockSpec((D, D), lambda i: (0, 0), pipeline_mode=pl.Buffered(1))
    return pl.pallas_call(
        _merge_kernel,
        grid=(n // tm,),
        in_specs=[row] * 7 + [wspec] * 4 + [pl.BlockSpec((1, D), lambda i: (0, 0))],
        out_specs=row,
        out_shape=jax.ShapeDtypeStruct((n, D), F32),
        compiler_params=_cparams(("parallel",)),
        name="merge",
    )(x, ya, za, yb, zb, ga, gb, wglu, wbs, wba, wout, gf.reshape(1, D))


def kernel(x_prompt, x_sample, cache_k, cache_v, page_table, state_ssm_re, state_ssm_im, norm_g, w_in,
           ssm_lambda_re, ssm_lambda_im, ssm_log_dt, ssm_b_re, ssm_b_im, ssm_c_re, ssm_c_im, ssm_d,
           w_glu, w_branch_ssm, w_branch_att, w_out, final_norm_g):
    depth = w_in.shape[0]
    assert depth == 1, "single-layer trunk"
    B, L, D = x_prompt.shape
    DB, T, _ = x_sample.shape
    past_len = page_table.shape[1] * PAGE_SIZE
    slopes = 2.0 ** (-8.0 * jnp.arange(1, N_HEADS + 1, dtype=F32) / N_HEADS)
    l = 0
    w_in_b = w_in[l].astype(BF16)
    wglu, wbs, wba, wout = (w[l].astype(BF16) for w in (w_glu, w_branch_ssm, w_branch_att, w_out))
    s5_params = (ssm_lambda_re[l], ssm_lambda_im[l], ssm_log_dt[l], ssm_b_re[l], ssm_b_im[l],
                 ssm_c_re[l], ssm_c_im[l], ssm_d[l])
    G = ssm_lambda_re.shape[1]

    t_chunk = 16
    u, za, q, k, v, zb, ga, gb, kb, vb, km = _in_proj(x_prompt, norm_g[l], w_in_b, tm=256, with_attn_copies=True)
    zeros = jnp.zeros((B, G, SSM_STATE), F32)
    ya, hr_p, hi_p = _s5_mix(u, _s5_weights(*s5_params, t_chunk), zeros, zeros, t_chunk)
    yb = _moba_prompt(q, km.reshape(B, L // MOBA_BLOCK, N_HEADS * HEAD_DIM), kb, vb, slopes)
    n = B * L
    r2 = lambda t: t.reshape(n, D)
    y_prompt = _merge(r2(x_prompt), r2(ya), r2(za), r2(yb), r2(zb), r2(ga), r2(gb),
                      wglu, wbs, wba, wout, final_norm_g, tm=512).reshape(B, L, D)

    ns = DB * T
    us, zas, qs, ks, vs, zbs, gas, gbs = _in_proj(x_sample.reshape(1, ns, D), norm_g[l], w_in_b, tm=256,
                                                  with_attn_copies=False)
    heads = lambda t: t.reshape(N_HEADS, DB, T, HEAD_DIM).transpose(1, 0, 2, 3)
    qs, ks, vs = heads(qs), heads(ks), heads(vs)
    yas, hr_s, hi_s = _s5_mix(us.reshape(DB, T, D), _s5_weights(*s5_params, T),
                              state_ssm_re[l], state_ssm_im[l], T)
    ybs = _moba_sample(qs, ks, vs, cache_k, cache_v, page_table, slopes, past_len)
    r2s = lambda t: t.reshape(ns, D)
    y_sample = _merge(r2s(x_sample), r2s(yas), r2s(zas), r2s(ybs), r2s(zbs), r2s(gas), r2s(gbs),
                      wglu, wbs, wba, wout, final_norm_g, tm=256).reshape(DB, T, D)

    return (y_prompt, y_sample, k[None], v[None], hr_p[None], hi_p[None], ks[None], vs[None], hr_s[None], hi_s[None])
```

```python
import functools
import math

import jax
import jax.numpy as jnp
from jax import lax
from jax.experimental import pallas as pl
from jax.experimental.pallas import tpu as pltpu

F32 = jnp.float32
BF16 = jnp.bfloat16

N_HEADS = 8
HEAD_DIM = 128
SSM_GROUP = 16
SSM_STATE = 64
MOBA_BLOCK = 256
MOBA_TOPK = 3
PAGE_SIZE = 128
NORM_EPS = 1e-6
NEG_INF = -1e30
N_SEG = 8
LANES = 128
GROUPS_PER_SLAB = LANES // SSM_GROUP
LOG2E = math.log2(math.e)

VMEM_LIMIT = 56 * 1024 * 1024
HIGHEST = lax.Precision.HIGHEST


def _cparams(sem):
    return pltpu.CompilerParams(dimension_semantics=sem, vmem_limit_bytes=VMEM_LIMIT)


def _sigmoid(x):
    return 1.0 / (1.0 + jnp.exp(-x))


def _silu(x):
    return x * _sigmoid(x)


def _gelu_tanh(x):
    c = math.sqrt(2.0 / math.pi)
    return 0.5 * x * (1.0 + jnp.tanh(c * (x + 0.044715 * (x * x * x))))


def _dot_nt(a, b, **kw):
    return lax.dot_general(a, b, (((1,), (1,)), ((), ())), preferred_element_type=F32, **kw)


def _bf16_trunc(x):
    bits = lax.bitcast_convert_type(x, jnp.uint32) & jnp.uint32(0xFFFF0000)
    return lax.bitcast_convert_type(bits, F32)


def _split3_bf16(x):
    hi = _bf16_trunc(x)
    mid = _bf16_trunc(x - hi)
    lo = x - hi - mid
    return hi.astype(BF16), mid.astype(BF16), lo.astype(BF16)


def _in_proj_kernel(x_ref, g_ref, w_ref, *refs, width, t_chunk, prompt):
    if prompt:
        xg_ref, za_ref, qt_ref, k_ref, v_ref, kb_ref, vt_ref, km_ref, zb_ref, ga_ref, gb_ref, u_scr = refs
    else:
        xg_ref, za_ref, q_ref, k_ref, v_ref, zb_ref, ga_ref, gb_ref, u_scr = refs
    x = x_ref[0]
    tm = x.shape[0]
    ms = jnp.mean(x * x, axis=-1, keepdims=True)
    xn = (x * lax.rsqrt(ms + NORM_EPS) * g_ref[...]).astype(BF16)

    def seg(s):
        return jnp.dot(xn, w_ref[:, s * width:(s + 1) * width], preferred_element_type=F32)

    def head(p, h):
        return p[:, h * HEAD_DIM:(h + 1) * HEAD_DIM]

    pu = seg(0)
    n_slabs = width // LANES
    for s in range(n_slabs):
        u_scr[s] = pu[:, s * LANES:(s + 1) * LANES]
    nc = tm // t_chunk
    for t in range(t_chunk):
        for s in range(n_slabs):
            rows_t = u_scr[s, pl.ds(t, nc, stride=t_chunk), :]
            for gl in range(GROUPS_PER_SLAB):
                xg_ref[s * GROUPS_PER_SLAB + gl, :, t * SSM_GROUP:(t + 1) * SSM_GROUP] = (
                    rows_t[:, gl * SSM_GROUP:(gl + 1) * SSM_GROUP].astype(xg_ref.dtype))
    za_ref[0] = seg(1).astype(za_ref.dtype)
    pq = seg(2)
    pk = seg(3)
    pv = seg(4)
    for h in range(N_HEADS):
        k_ref[0, h] = head(pk, h)
        v_ref[0, h] = head(pv, h)
    if prompt:
        i = pl.program_id(1)
        for h in range(N_HEADS):
            for blk in range(tm // MOBA_BLOCK):
                rows = slice(blk * MOBA_BLOCK, (blk + 1) * MOBA_BLOCK)
                qt_ref[0, h, blk] = head(pq, h)[rows].T
                vt_ref[0, h, blk] = head(pv, h)[rows].T.astype(BF16)
            kb_ref[0, h] = head(pk, h).astype(BF16)
        for blk in range(tm // MOBA_BLOCK):
            km_ref[0, pl.ds(i * (tm // MOBA_BLOCK) + blk, 1), :] = jnp.mean(
                pk[blk * MOBA_BLOCK:(blk + 1) * MOBA_BLOCK], axis=0, keepdims=True)
    else:
        for h in range(N_HEADS):
            q_ref[0, h] = head(pq, h)
    zb_ref[0] = seg(5).astype(zb_ref.dtype)
    ga_ref[0] = seg(6).astype(ga_ref.dtype)
    gb_ref[0] = seg(7).astype(gb_ref.dtype)


def _in_proj(x, g, w_bf16, *, tm, t_chunk, prompt):
    nb, L, D = x.shape
    width = w_bf16.shape[1] // N_SEG
    assert width == N_HEADS * HEAD_DIM and L % tm == 0 and tm % t_chunk == 0
    G = width // SSM_GROUP
    nt = L // tm
    flat = lambda dt: jax.ShapeDtypeStruct((nb, L, width), dt)
    heads = lambda dt: jax.ShapeDtypeStruct((nb, N_HEADS, L, HEAD_DIM), dt)
    flat_spec = pl.BlockSpec((1, tm, width), lambda b, i: (b, i, 0))
    head_spec = pl.BlockSpec((1, N_HEADS, tm, HEAD_DIM), lambda b, i: (b, 0, i, 0))
    xg_shape = jax.ShapeDtypeStruct((G, nb * L // t_chunk, t_chunk * SSM_GROUP), BF16)
    xg_spec = pl.BlockSpec((G, tm // t_chunk, t_chunk * SSM_GROUP), lambda b, i: (0, b * nt + i, 0))
    if prompt:
        assert tm % MOBA_BLOCK == 0
        nblk = tm // MOBA_BLOCK
        n_blocks = L // MOBA_BLOCK
        tr = lambda dt: jax.ShapeDtypeStruct((nb, N_HEADS, n_blocks, HEAD_DIM, MOBA_BLOCK), dt)
        tr_spec = pl.BlockSpec((1, N_HEADS, nblk, HEAD_DIM, MOBA_BLOCK), lambda b, i: (b, 0, i, 0, 0))
        out_shape = [xg_shape, flat(BF16), tr(F32), heads(F32), heads(F32), heads(BF16), tr(BF16),
                     jax.ShapeDtypeStruct((nb, n_blocks, width), F32), flat(BF16), flat(BF16), flat(BF16)]
        out_specs = [xg_spec, flat_spec, tr_spec, head_spec, head_spec, head_spec, tr_spec,
                     pl.BlockSpec((1, n_blocks, width), lambda b, i: (b, 0, 0)), flat_spec, flat_spec, flat_spec]
    else:
        out_shape = [xg_shape, flat(BF16), heads(F32), heads(F32), heads(F32), flat(BF16), flat(BF16), flat(BF16)]
        out_specs = [xg_spec, flat_spec, head_spec, head_spec, head_spec, flat_spec, flat_spec, flat_spec]
    return pl.pallas_call(
        functools.partial(_in_proj_kernel, width=width, t_chunk=t_chunk, prompt=prompt),
        grid=(nb, nt),
        in_specs=[
            pl.BlockSpec((1, tm, D), lambda b, i: (b, i, 0)),
            pl.BlockSpec((1, D), lambda b, i: (0, 0)),
            pl.BlockSpec(w_bf16.shape, lambda b, i: (0, 0), pipeline_mode=pl.Buffered(1)),
        ],
        out_specs=out_specs,
        out_shape=out_shape,
        scratch_shapes=[pltpu.VMEM((width // LANES, tm, LANES), F32)],
        compiler_params=_cparams(("parallel", "arbitrary")),
        name="in_proj",
    )(x, g.reshape(1, D), w_bf16)


def _s5_weights(lam_re, lam_im, log_dt, b_re, b_im, c_re, c_im, d_skip, T):
    G, P = lam_re.shape
    H = SSM_GROUP
    dt = jnp.exp(log_dt.astype(F32))[:, None]
    lr = lam_re.astype(F32)
    li = lam_im.astype(F32)
    mag = jnp.exp(lr * dt)
    a_re = mag * jnp.cos(li * dt)
    a_im = mag * jnp.sin(li * dt)
    den = lr * lr + li * li
    coef_re = ((a_re - 1.0) * lr + a_im * li) / den
    coef_im = (a_im * lr - (a_re - 1.0) * li) / den
    br = b_re.astype(F32)
    bi = b_im.astype(F32)
    bbar_re = coef_re[..., None] * br - coef_im[..., None] * bi
    bbar_im = coef_re[..., None] * bi + coef_im[..., None] * br
    j = jnp.arange(T + 1, dtype=F32)[:, None, None]
    pm = jnp.exp(lr * dt * j)
    pr = pm * jnp.cos(li * dt * j)
    pi = pm * jnp.sin(li * dt * j)
    apr = pr[..., None] * bbar_re - pi[..., None] * bbar_im
    api = pr[..., None] * bbar_im + pi[..., None] * bbar_re
    cr = c_re.astype(F32)
    ci = c_im.astype(F32)
    kj = (jnp.einsum('ghp,jgpk->jghk', cr, apr[:T], precision=HIGHEST)
          - jnp.einsum('ghp,jgpk->jghk', ci, api[:T], precision=HIGHEST))
    lag = jnp.arange(T)[None, :] - jnp.arange(T)[:, None]
    kt = jnp.where((lag >= 0)[:, :, None, None, None], kj[jnp.clip(lag, 0, T - 1)], 0.0)
    toep = kt.transpose(2, 0, 4, 1, 3)
    eye_t = jnp.eye(T, dtype=F32)
    eye_h = jnp.eye(H, dtype=F32)
    dmat = d_skip.astype(F32).reshape(G, H)
    toep = toep + (eye_t[None, :, None, :, None] * eye_h[None, None, :, None, :]
                   * dmat[:, None, None, None, :])
    toep = toep.reshape(G, T * H, T * H)
    wr = apr[:T][::-1].transpose(1, 0, 3, 2)
    wi = api[:T][::-1].transpose(1, 0, 3, 2)
    TH = T * H
    wr = wr.reshape(G // 2, 2, TH, P)
    wi = wi.reshape(G // 2, 2, TH, P)
    z = jnp.zeros_like(wr[:, 0])
    wst = jnp.stack([
        jnp.concatenate([wr[:, 0], z, wi[:, 0], z], axis=-1),
        jnp.concatenate([z, wr[:, 1], z, wi[:, 1]], axis=-1)], axis=1)
    p1r = pr[1:]
    p1i = pi[1:]
    car = cr[None] * p1r[:, :, None, :] - ci[None] * p1i[:, :, None, :]
    cai = cr[None] * p1i[:, :, None, :] + ci[None] * p1r[:, :, None, :]
    from_re = car.transpose(1, 3, 0, 2).reshape(G // 2, 2, P, TH)
    from_im = (-cai).transpose(1, 3, 0, 2).reshape(G // 2, 2, P, TH)
    zc = jnp.zeros_like(from_re[:, 0])
    cst = jnp.concatenate([
        jnp.concatenate([from_re[:, 0], zc], axis=-1),
        jnp.concatenate([zc, from_re[:, 1]], axis=-1),
        jnp.concatenate([from_im[:, 0], zc], axis=-1),
        jnp.concatenate([zc, from_im[:, 1]], axis=-1)], axis=1)
    at = jnp.stack([pr[T].reshape(G // 2, 2 * P), pi[T].reshape(G // 2, 2 * P)], axis=1)
    return toep.astype(BF16), wst.astype(BF16), cst.astype(BF16), at


def _s5_kernel(x_ref, toep_ref, wst_ref, cst_ref, at_ref, h0r_ref, h0i_ref,
               y_ref, hr_ref, hi_ref, s_scr, hst_scr, *, n_chunks, nb, th):
    x0 = x_ref[0]
    x1 = x_ref[1]
    sw = 2 * SSM_STATE
    s = (jnp.dot(x0, wst_ref[0, 0], preferred_element_type=F32)
         + jnp.dot(x1, wst_ref[0, 1], preferred_element_type=F32))
    s_scr[0] = s[:, :sw]
    s_scr[1] = s[:, sw:]
    ar = jnp.broadcast_to(at_ref[0, 0:1, :], (nb, sw))
    ai = jnp.broadcast_to(at_ref[0, 1:2, :], (nb, sw))

    def chunk_step(c, carry):
        hr, hi = carry
        rows = pl.ds(c, nb, stride=n_chunks) if n_chunks > 1 else pl.ds(0, nb)
        hst_scr[0, rows, :] = hr
        hst_scr[1, rows, :] = hi
        sr = s_scr[0, rows, :]
        si = s_scr[1, rows, :]
        return ar * hr - ai * hi + sr, ar * hi + ai * hr + si

    hr, hi = lax.fori_loop(0, n_chunks, chunk_step, (h0r_ref[0], h0i_ref[0]))
    hr_ref[0] = hr
    hi_ref[0] = hi
    hst = jnp.concatenate([hst_scr[0], hst_scr[1]], axis=1).astype(BF16)
    ys = jnp.dot(hst, cst_ref[0], preferred_element_type=F32)
    y0 = jnp.dot(x0, toep_ref[0], preferred_element_type=F32)
    y1 = jnp.dot(x1, toep_ref[1], preferred_element_type=F32)
    y_ref[0] = (y0 + ys[:, :th]).astype(y_ref.dtype)
    y_ref[1] = (y1 + ys[:, th:]).astype(y_ref.dtype)


def _s5_mix(xg, weights, h0_re, h0_im, T):
    toep, wst, cst, at = weights
    G, rows, th = xg.shape
    nb = h0_re.shape[0]
    P = SSM_STATE
    np_ = G // 2
    C = rows // nb
    assert th == T * SSM_GROUP and rows == nb * C
    h0r = h0_re.reshape(nb, np_, 2 * P).transpose(1, 0, 2)
    h0i = h0_im.reshape(nb, np_, 2 * P).transpose(1, 0, 2)
    y, hr, hi = pl.pallas_call(
        functools.partial(_s5_kernel, n_chunks=C, nb=nb, th=th),
        grid=(np_,),
        in_specs=[
            pl.BlockSpec((2, rows, th), lambda p: (p, 0, 0)),
            pl.BlockSpec((2, th, th), lambda p: (p, 0, 0)),
            pl.BlockSpec((1, 2, th, 4 * P), lambda p: (p, 0, 0, 0)),
            pl.BlockSpec((1, 4 * P, 2 * th), lambda p: (p, 0, 0)),
            pl.BlockSpec((1, 2, 2 * P), lambda p: (p, 0, 0)),
            pl.BlockSpec((1, nb, 2 * P), lambda p: (p, 0, 0)),
            pl.BlockSpec((1, nb, 2 * P), lambda p: (p, 0, 0)),
        ],
        out_specs=[
            pl.BlockSpec((2, rows, th), lambda p: (p, 0, 0)),
            pl.BlockSpec((1, nb, 2 * P), lambda p: (p, 0, 0)),
            pl.BlockSpec((1, nb, 2 * P), lambda p: (p, 0, 0)),
        ],
        out_shape=[
            jax.ShapeDtypeStruct((G, rows, th), BF16),
            jax.ShapeDtypeStruct((np_, nb, 2 * P), F32),
            jax.ShapeDtypeStruct((np_, nb, 2 * P), F32),
        ],
        scratch_shapes=[pltpu.VMEM((2, rows, 2 * P), F32), pltpu.VMEM((2, rows, 2 * P), F32)],
        compiler_params=_cparams(("parallel",)),
        name="s5_mix",
    )(xg, toep, wst, cst, at, h0r, h0i)
    hr = hr.transpose(1, 0, 2).reshape(nb, G, P)
    hi = hi.transpose(1, 0, 2).reshape(nb, G, P)
    return y, hr, hi


AUG_ROWS = 16
SLOTS = 4


def _moba_prompt_kernel(qt_ref, qn_ref, kme_ref, kb_ref, vt_ref, augc_ref, sl_ref, o_ref,
                        mask_scr, aug_scr, sa_scr, sb_scr, *, n_blocks):
    i = pl.program_id(2)
    bs = MOBA_BLOCK
    cur = i % 2
    qs = (qt_ref[0, 0, 0] * (HEAD_DIM ** -0.5 * LOG2E)).astype(BF16)
    augc = augc_ref[0]
    slope_blk = sl_ref[0][:, :1]
    zpad = jnp.zeros((HEAD_DIM - AUG_ROWS, bs), BF16)
    r16 = lax.broadcasted_iota(jnp.int32, (AUG_ROWS, bs), 0)
    ones_rows = jnp.where((r16 >= SLOTS) & (r16 < SLOTS + 3), 1.0, 0.0)

    def scores(kb_rows, aug_tile, n_slots):
        rhs = jnp.concatenate([qs, aug_tile.astype(BF16), zpad], axis=0)
        lhs = jnp.concatenate([kb_rows, augc[:n_slots * bs]], axis=1)
        return jnp.dot(lhs, rhs, preferred_element_type=F32)

    n_trips_max = n_blocks // SLOTS

    @pl.when(i == 0)
    def _():
        for t in range(n_trips_max):
            aug_scr[cur, t] = ones_rows

    def past_scores(t, dst_ref):
        tc = jnp.minimum(t, n_trips_max - 1)
        r0 = pl.multiple_of(tc * (SLOTS * bs), SLOTS * bs)
        dst_ref[...] = scores(kb_ref[0, 0, pl.ds(r0, SLOTS * bs), :], aug_scr[cur, tc], SLOTS)

    past_scores(0, sa_scr)
    s_own = scores(kb_ref[0, 0, pl.ds(pl.multiple_of(i * bs, bs), bs), :], ones_rows, 1)

    i_next = i + 1
    means = kme_ref[0]
    st = jnp.dot(means, qn_ref[0, 0, 0], precision=HIGHEST, preferred_element_type=F32)
    jrow = lax.broadcasted_iota(jnp.int32, st.shape, 0)
    rank = jnp.zeros(st.shape, jnp.int32)
    for jp in range(n_blocks - 1):
        row = st[jp:jp + 1, :]
        beats = ((row > st) | ((row == st) & (jp < jrow))) & (jp < i_next)
        rank = rank + beats.astype(jnp.int32)
    chosen = (rank < MOBA_TOPK) & (jrow < i_next)
    mask_scr[...] = jnp.where(chosen, 0.0, NEG_INF)
    for t in range(n_trips_max):
        aug_scr[1 - cur, t] = ones_rows
        aug_scr[1 - cur, t, 0:SLOTS, :] = mask_scr[t * SLOTS:(t + 1) * SLOTS, :]

    def attend(s2, blk, beta, carry):
        m, l, acc = carry
        m_new = jnp.maximum(m, jnp.max(s2, axis=0, keepdims=True) + beta)
        alpha = jnp.exp2(m - m_new)
        p = jnp.exp2(s2 - (m_new - beta))
        l = alpha * l + jnp.sum(p, axis=0, keepdims=True)
        acc = alpha * acc + jnp.dot(vt_ref[0, 0, blk], p.astype(BF16), preferred_element_type=F32)
        return m_new, l, acc

    def past_trip(t, s_ref, carry):
        for u in range(SLOTS):
            blk = t * SLOTS + u
            beta = slope_blk * (blk - i).astype(F32)
            carry = attend(s_ref[u * bs:(u + 1) * bs, :], blk, beta, carry)
        return carry

    keys = lax.broadcasted_iota(jnp.int32, (bs, bs), 0)
    queries = lax.broadcasted_iota(jnp.int32, (bs, bs), 1)
    init = (jnp.full((1, bs), NEG_INF, F32), jnp.zeros((1, bs), F32), jnp.zeros((HEAD_DIM, bs), F32))
    carry = attend(jnp.where(keys <= queries, s_own, NEG_INF), i, 0.0, init)
    n_trips = (i + SLOTS - 1) // SLOTS

    def trip_pair(k, c):
        t0 = 2 * k
        past_scores(t0 + 1, sb_scr)
        c = past_trip(t0, sa_scr, c)
        past_scores(t0 + 2, sa_scr)
        return past_trip(t0 + 1, sb_scr, c)

    carry = lax.fori_loop(0, n_trips // 2, trip_pair, carry)
    m, l, acc = lax.cond(n_trips % 2 == 1, lambda c: past_trip(n_trips - 1, sa_scr, c), lambda c: c, carry)
    o_ref[0] = (acc / l).T.astype(o_ref.dtype)


def _moba_prompt(qt, kmeans, kb, vt, slopes):
    nb, H, n_blocks, Dh, bs = qt.shape
    L = n_blocks * bs
    assert n_blocks % SLOTS == 0 and n_blocks <= AUG_ROWS and SLOTS + 3 <= AUG_ROWS
    c = jnp.arange(bs, dtype=F32)
    bias = slopes[:, None] * c[None, :] * LOG2E
    pieces = jnp.stack(_split3_bf16(bias), axis=-1)
    onehot = jnp.eye(SLOTS, dtype=BF16)
    augc = jnp.concatenate([
        jnp.broadcast_to(onehot[None, :, None, :], (H, SLOTS, bs, SLOTS)),
        jnp.broadcast_to(pieces[:, None], (H, SLOTS, bs, 3)),
        jnp.zeros((H, SLOTS, bs, Dh - SLOTS - 3), BF16)], axis=-1).reshape(H, SLOTS * bs, Dh)
    sl = jnp.broadcast_to((slopes * (bs * LOG2E))[:, None, None], (H, 1, LANES))
    return pl.pallas_call(
        functools.partial(_moba_prompt_kernel, n_blocks=n_blocks),
        grid=(nb, H, n_blocks),
        in_specs=[
            pl.BlockSpec((1, 1, 1, Dh, bs), lambda b, h, i: (b, h, i, 0, 0)),
            pl.BlockSpec((1, 1, 1, Dh, bs), lambda b, h, i: (b, h, jnp.minimum(i + 1, n_blocks - 1), 0, 0)),
            pl.BlockSpec((1, n_blocks, Dh), lambda b, h, i: (b, 0, h)),
            pl.BlockSpec((1, 1, L, Dh), lambda b, h, i: (b, h, 0, 0)),
            pl.BlockSpec((1, 1, n_blocks, Dh, bs), lambda b, h, i: (b, h, 0, 0, 0)),
            pl.BlockSpec((1, SLOTS * bs, Dh), lambda b, h, i: (h, 0, 0)),
            pl.BlockSpec((1, 1, LANES), lambda b, h, i: (h, 0, 0)),
        ],
        out_specs=pl.BlockSpec((1, bs, Dh), lambda b, h, i: (b, i, h)),
        out_shape=jax.ShapeDtypeStruct((nb, L, H * Dh), BF16),
        scratch_shapes=[pltpu.VMEM((n_blocks, bs), F32), pltpu.VMEM((2, n_blocks // SLOTS, AUG_ROWS, bs), F32),
                        pltpu.VMEM((SLOTS * bs, bs), F32), pltpu.VMEM((SLOTS * bs, bs), F32)],
        compiler_params=_cparams(("parallel", "parallel", "arbitrary")),
        name="moba_prompt",
    )(qt, qt, kmeans, kb, vt, augc, sl)


PAGES_PER_STEP = 8
BLOCKS_PER_STEP = PAGES_PER_STEP * PAGE_SIZE // MOBA_BLOCK


def _page_specs(n_pages):
    def spec(e):
        return pl.BlockSpec((1, 1, N_HEADS, PAGE_SIZE, HEAD_DIM),
                            lambda s, g, pt: (0, pt[s * n_pages + g * PAGES_PER_STEP + e], 0, 0, 0))
    return [spec(e) for e in range(PAGES_PER_STEP)]


def _block_from_pages(page_refs, e2, h):
    ppb = MOBA_BLOCK // PAGE_SIZE
    return jnp.concatenate([page_refs[ppb * e2 + t][0, 0, h] for t in range(ppb)], axis=0)


def _topk_rank_cols(s, n_cand):
    jidx = lax.broadcasted_iota(jnp.int32, s.shape, 1)
    rank = jnp.zeros(s.shape, jnp.int32)
    for jp in range(n_cand):
        col = s[:, jp:jp + 1]
        beats = (col > s) | ((col == s) & (jp < jidx))
        rank = rank + beats.astype(jnp.int32)
    return rank


def _moba_sample_probs_kernel(pt_ref, qbd_ref, kn_ref, b0_ref, slb_ref, bo_ref, *refs, n_past_blocks):
    k_pages = refs[:PAGES_PER_STEP]
    p_ref, po_ref, inv_ref, sc_scr = refs[PAGES_PER_STEP:]
    g = pl.program_id(1)
    bs = MOBA_BLOCK
    qbd = qbd_ref[0]
    rows = qbd.shape[0]
    lane = lax.broadcasted_iota(jnp.int32, (rows, LANES), 1)

    @pl.when(g == 0)
    def _():
        sc_scr[...] = jnp.zeros_like(sc_scr)

    for e2 in range(BLOCKS_PER_STEP):
        blk = g * BLOCKS_PER_STEP + e2
        kcat = jnp.concatenate([_block_from_pages(k_pages, e2, h).astype(BF16) for h in range(N_HEADS)], axis=1)
        lg = _dot_nt(qbd, kcat)
        p_ref[0, blk] = lg
        sc_scr[...] = jnp.where(lane == blk, jnp.sum(lg, axis=1, keepdims=True), sc_scr[...])

    @pl.when(g == pl.num_programs(1) - 1)
    def _():
        rank = _topk_rank_cols(sc_scr[...], n_past_blocks)
        sel = rank < MOBA_TOPK
        tpad = kn_ref.shape[1]
        kn = jnp.concatenate([kn_ref[0], jnp.zeros((LANES - tpad, kn_ref.shape[2]), F32)], axis=0).astype(BF16)
        l_own = _dot_nt(qbd, kn) + bo_ref[...]
        m = jnp.max(l_own, axis=1, keepdims=True)
        b0 = b0_ref[...]
        slb = slb_ref[...]
        for b in range(n_past_blocks):
            lb = jnp.where(sel[:, b:b + 1], p_ref[0, b] + (b0 + slb * float(b)), NEG_INF)
            p_ref[0, b] = lb
            m = jnp.maximum(m, jnp.max(lb, axis=1, keepdims=True))
        p_own = jnp.exp(l_own - m)
        den = jnp.sum(p_own, axis=1, keepdims=True)
        po_ref[0] = p_own
        for b in range(n_past_blocks):
            pb = jnp.exp(p_ref[0, b] - m)
            p_ref[0, b] = pb
            den = den + jnp.sum(pb, axis=1, keepdims=True)
        inv_ref[0] = jnp.broadcast_to(1.0 / den, (rows, LANES))


def _moba_sample_pv_kernel(pt_ref, p_ref, po_ref, inv_ref, vn_ref, *refs):
    v_pages = refs[:PAGES_PER_STEP]
    o_ref, acc_scr = refs[PAGES_PER_STEP:]
    g = pl.program_id(1)
    T = vn_ref.shape[1]
    hrows = lambda h: slice(h * T, (h + 1) * T)
    hcols = lambda h: slice(h * HEAD_DIM, (h + 1) * HEAD_DIM)

    @pl.when(g == 0)
    def _():
        pad = jnp.zeros((LANES - T, HEAD_DIM), F32)
        for h in range(N_HEADS):
            vn = jnp.concatenate([vn_ref[0, :, hcols(h)], pad], axis=0).astype(BF16)
            acc_scr[h] = jnp.dot(po_ref[0, hrows(h), :].astype(BF16), vn, preferred_element_type=F32)

    for e2 in range(BLOCKS_PER_STEP):
        for h in range(N_HEADS):
            vblk = _block_from_pages(v_pages, e2, h).astype(BF16)
            acc_scr[h] += jnp.dot(p_ref[0, e2, hrows(h), :].astype(BF16), vblk, preferred_element_type=F32)

    @pl.when(g == pl.num_programs(1) - 1)
    def _():
        for h in range(N_HEADS):
            o_ref[0, :, hcols(h)] = (acc_scr[h] * inv_ref[0, hrows(h), :]).astype(o_ref.dtype)


def _moba_sample(q, k_new, v_new, cache_k, cache_v, page_table, slopes, past_len):
    ndb, H, T, Dh = q.shape
    n_pages = page_table.shape[1]
    assert past_len % MOBA_BLOCK == 0 and n_pages * PAGE_SIZE == past_len and n_pages % PAGES_PER_STEP == 0
    n_past_blocks = past_len // MOBA_BLOCK
    assert n_past_blocks <= LANES and T <= LANES
    n_steps = n_pages // PAGES_PER_STEP
    bs = MOBA_BLOCK
    rows = H * T
    pt = page_table.reshape(-1).astype(jnp.int32)
    qbd = ((q * (Dh ** -0.5))[:, :, :, None, :] * jnp.eye(H, dtype=F32)[None, :, None, :, None])
    qbd = qbd.reshape(ndb, rows, H * Dh).astype(BF16)
    slope_row = jnp.repeat(slopes, T)[:, None]
    t_row = jnp.tile(jnp.arange(T, dtype=F32), H)[:, None]
    c = jnp.arange(bs, dtype=F32)[None, :]
    b0 = -slope_row * (past_len + t_row - c)
    slb = jnp.broadcast_to(slope_row * bs, (rows, bs))
    tc = jnp.arange(LANES, dtype=F32)[None, :]
    bo = jnp.where(tc <= t_row, -slope_row * (t_row - tc), NEG_INF)
    const = lambda shape: pl.BlockSpec(shape, lambda s, g, pt: (0,) * len(shape))
    probs, p_own, inv = pl.pallas_call(
        functools.partial(_moba_sample_probs_kernel, n_past_blocks=n_past_blocks),
        grid_spec=pltpu.PrefetchScalarGridSpec(
            num_scalar_prefetch=1,
            grid=(ndb, n_steps),
            in_specs=[
                pl.BlockSpec((1, rows, H * Dh), lambda s, g, pt: (s, 0, 0)),
                pl.BlockSpec((1, T, H * Dh), lambda s, g, pt: (s, 0, 0)),
                const((rows, bs)), const((rows, bs)), const((rows, LANES)),
            ] + _page_specs(n_pages),
            out_specs=[
                pl.BlockSpec((1, n_past_blocks, rows, bs), lambda s, g, pt: (s, 0, 0, 0)),
                pl.BlockSpec((1, rows, LANES), lambda s, g, pt: (s, 0, 0)),
                pl.BlockSpec((1, rows, LANES), lambda s, g, pt: (s, 0, 0)),
            ],
            scratch_shapes=[pltpu.VMEM((rows, LANES), F32)],
        ),
        out_shape=[
            jax.ShapeDtypeStruct((ndb, n_past_blocks, rows, bs), F32),
            jax.ShapeDtypeStruct((ndb, rows, LANES), F32),
            jax.ShapeDtypeStruct((ndb, rows, LANES), F32),
        ],
        compiler_params=_cparams(("parallel", "arbitrary")),
        name="moba_sample_probs",
    )(pt, qbd, k_new, b0, slb, bo, *([cache_k] * PAGES_PER_STEP))
    return pl.pallas_call(
        _moba_sample_pv_kernel,
        grid_spec=pltpu.PrefetchScalarGridSpec(
            num_scalar_prefetch=1,
            grid=(ndb, n_steps),
            in_specs=[
                pl.BlockSpec((1, BLOCKS_PER_STEP, rows, bs), lambda s, g, pt: (s, g, 0, 0)),
                pl.BlockSpec((1, rows, LANES), lambda s, g, pt: (s, 0, 0)),
                pl.BlockSpec((1, rows, LANES), lambda s, g, pt: (s, 0, 0)),
                pl.BlockSpec((1, T, H * Dh), lambda s, g, pt: (s, 0, 0)),
            ] + _page_specs(n_pages),
            out_specs=pl.BlockSpec((1, T, H * Dh), lambda s, g, pt: (s, 0, 0)),
            scratch_shapes=[pltpu.VMEM((H, T, Dh), F32)],
        ),
        out_shape=jax.ShapeDtypeStruct((ndb, T, H * Dh), BF16),
        compiler_params=_cparams(("parallel", "arbitrary")),
        name="moba_sample_pv",
    )(pt, probs, p_own, inv, v_new, *([cache_v] * PAGES_PER_STEP))


def _merge_kernel(x_ref, yg_ref, za_ref, yb_ref, zb_ref, ga_ref, gb_ref,
                  wglu_ref, wbs_ref, wba_ref, wout_ref, gf_ref, o_ref, ya_scr, *, t_chunk):
    f = lambda r: r[...].astype(F32)
    n_slabs = ya_scr.shape[0]
    nc = yg_ref.shape[1]
    for t in range(t_chunk):
        for s in range(n_slabs):
            rows_t = jnp.concatenate(
                [yg_ref[s * GROUPS_PER_SLAB + gl, :, t * SSM_GROUP:(t + 1) * SSM_GROUP].astype(F32)
                 for gl in range(GROUPS_PER_SLAB)], axis=1)
            ya_scr[s, pl.ds(t, nc, stride=t_chunk), :] = rows_t
    ya = jnp.concatenate([ya_scr[s] for s in range(n_slabs)], axis=1)
    s = _gelu_tanh(ya)
    s = s * _sigmoid(jnp.dot(s.astype(BF16), wglu_ref[...], preferred_element_type=F32))
    s = s * _silu(f(za_ref))
    a = f(yb_ref) * _silu(f(zb_ref))
    mixed = (_sigmoid(f(ga_ref)) * jnp.dot(s.astype(BF16), wbs_ref[...], preferred_element_type=F32)
             + _sigmoid(f(gb_ref)) * jnp.dot(a.astype(BF16), wba_ref[...], preferred_element_type=F32))
    o = x_ref[...] + jnp.dot(mixed.astype(BF16), wout_ref[...], preferred_element_type=F32)
    ms = jnp.mean(o * o, axis=-1, keepdims=True)
    o_ref[...] = o * lax.rsqrt(ms + NORM_EPS) * gf_ref[...]


def _merge(x, yg, za, yb, zb, ga, gb, wglu, wbs, wba, wout, gf, *, tm, t_chunk):
    n, D = x.shape
    G, _, th = yg.shape
    assert n % tm == 0 and tm % t_chunk == 0 and th == t_chunk * SSM_GROUP
    row = pl.BlockSpec((tm, D), lambda i: (i, 0))
    wspec = pl.BlockSpec((D, D), lambda i: (0, 0), pipeline_mode=pl.Buffered(1))
    return pl.pallas_call(
        functools.partial(_merge_kernel, t_chunk=t_chunk),
        grid=(n // tm,),
        in_specs=[row, pl.BlockSpec((G, tm // t_chunk, th), lambda i: (0, i, 0))] + [row] * 5 + [wspec] * 4
        + [pl.BlockSpec((1, D), lambda i: (0, 0))],
        out_specs=row,
        out_shape=jax.ShapeDtypeStruct((n, D), F32),
        scratch_shapes=[pltpu.VMEM((D // LANES, tm, LANES), F32)],
        compiler_params=_cparams(("parallel",)),
        name="merge",
    )(x, yg, za, yb, zb, ga, gb, wglu, wbs, wba, wout, gf.reshape(1, D))


def kernel(x_prompt, x_sample, cache_k, cache_v, page_table, state_ssm_re, state_ssm_im, norm_g, w_in,
           ssm_lambda_re, ssm_lambda_im, ssm_log_dt, ssm_b_re, ssm_b_im, ssm_c_re, ssm_c_im, ssm_d,
           w_glu, w_branch_ssm, w_branch_att, w_out, final_norm_g):
    depth = w_in.shape[0]
    assert depth == 1, "single-layer trunk"
    B, L, D = x_prompt.shape
    DB, T, _ = x_sample.shape
    past_len = page_table.shape[1] * PAGE_SIZE
    slopes = 2.0 ** (-8.0 * jnp.arange(1, N_HEADS + 1, dtype=F32) / N_HEADS)
    l = 0
    w_in_b = w_in[l].astype(BF16)
    wglu, wbs, wba, wout = (w[l].astype(BF16) for w in (w_glu, w_branch_ssm, w_branch_att, w_out))
    s5_params = (ssm_lambda_re[l], ssm_lambda_im[l], ssm_log_dt[l], ssm_b_re[l], ssm_b_im[l],
                 ssm_c_re[l], ssm_c_im[l], ssm_d[l])
    G = ssm_lambda_re.shape[1]

    t_chunk = 16
    xg, za, qt, k, v, kb, vt, km, zb, ga, gb = _in_proj(x_prompt, norm_g[l], w_in_b, tm=256, t_chunk=t_chunk,
                                                        prompt=True)
    zeros = jnp.zeros((B, G, SSM_STATE), F32)
    yg, hr_p, hi_p = _s5_mix(xg, _s5_weights(*s5_params, t_chunk), zeros, zeros, t_chunk)
    yb = _moba_prompt(qt, km, kb, vt, slopes)
    n = B * L
    r2 = lambda t: t.reshape(n, D)
    y_prompt = _merge(r2(x_prompt), yg, r2(za), r2(yb), r2(zb), r2(ga), r2(gb),
                      wglu, wbs, wba, wout, final_norm_g, tm=512, t_chunk=t_chunk).reshape(B, L, D)

    ns = DB * T
    xgs, zas, qs, ks, vs, zbs, gas, gbs = _in_proj(x_sample.reshape(1, ns, D), norm_g[l], w_in_b, tm=256,
                                                   t_chunk=T, prompt=False)
    heads = lambda t: t.reshape(N_HEADS, DB, T, HEAD_DIM).transpose(1, 0, 2, 3)
    qs, ks, vs = heads(qs), heads(ks), heads(vs)
    nat = lambda t: t.transpose(0, 2, 1, 3).reshape(DB, T, N_HEADS * HEAD_DIM)
    ygs, hr_s, hi_s = _s5_mix(xgs, _s5_weights(*s5_params, T), state_ssm_re[l], state_ssm_im[l], T)
    ybs = _moba_sample(qs, nat(ks), nat(vs), cache_k, cache_v, page_table, slopes, past_len)
    r2s = lambda t: t.reshape(ns, D)
    y_sample = _merge(r2s(x_sample), ygs, r2s(zas), r2s(ybs), r2s(zbs), r2s(gas), r2s(gbs),
                      wglu, wbs, wba, wout, final_norm_g, tm=256, t_chunk=T).reshape(DB, T, D)

    return (y_prompt, y_sample, k[None], v[None], hr_p[None], hi_p[None], ks[None], vs[None], hr_s[None], hi_s[None])
```

```python
import functools
import math

import jax
import jax.numpy as jnp
from jax import lax
from jax.experimental import pallas as pl
from jax.experimental.pallas import tpu as pltpu

F32 = jnp.float32
BF16 = jnp.bfloat16

N_HEADS = 8
HEAD_DIM = 128
SSM_GROUP = 16
SSM_STATE = 64
MOBA_BLOCK = 256
MOBA_TOPK = 3
PAGE_SIZE = 128
NORM_EPS = 1e-6
NEG_INF = -1e30
N_SEG = 8
LANES = 128
GROUPS_PER_SLAB = LANES // SSM_GROUP
LOG2E = math.log2(math.e)

VMEM_LIMIT = 56 * 1024 * 1024
HIGHEST = lax.Precision.HIGHEST


def _cparams(sem):
    return pltpu.CompilerParams(dimension_semantics=sem, vmem_limit_bytes=VMEM_LIMIT)


def _sigmoid(x):
    return 0.5 * jnp.tanh(0.5 * x) + 0.5


def _silu(x):
    return x * _sigmoid(x)


def _gelu_tanh(x):
    c = math.sqrt(2.0 / math.pi)
    return 0.5 * x * (1.0 + jnp.tanh(c * (x + 0.044715 * (x * x * x))))


def _dot_nt(a, b, **kw):
    return lax.dot_general(a, b, (((1,), (1,)), ((), ())), preferred_element_type=F32, **kw)


def _bf16_trunc(x):
    bits = lax.bitcast_convert_type(x, jnp.uint32) & jnp.uint32(0xFFFF0000)
    return lax.bitcast_convert_type(bits, F32)


def _split3_bf16(x):
    hi = _bf16_trunc(x)
    mid = _bf16_trunc(x - hi)
    lo = x - hi - mid
    return hi.astype(BF16), mid.astype(BF16), lo.astype(BF16)


def _in_proj_kernel(x_ref, g_ref, w_ref, *refs, width, t_chunk, prompt):
    if prompt:
        xg_ref, za_ref, qt_ref, k_ref, v_ref, kb_ref, vt_ref, km_ref, zb_ref, ga_ref, gb_ref, u_scr = refs
    else:
        xg_ref, za_ref, q_ref, k_ref, v_ref, zb_ref, ga_ref, gb_ref, u_scr = refs
    x = x_ref[0]
    tm = x.shape[0]
    ms = jnp.mean(x * x, axis=-1, keepdims=True)
    xn = (x * lax.rsqrt(ms + NORM_EPS) * g_ref[...]).astype(BF16)

    def seg(s):
        return jnp.dot(xn, w_ref[:, s * width:(s + 1) * width], preferred_element_type=F32)

    def head(p, h):
        return p[:, h * HEAD_DIM:(h + 1) * HEAD_DIM]

    pu = seg(0)
    n_slabs = width // LANES
    for s in range(n_slabs):
        u_scr[s] = pu[:, s * LANES:(s + 1) * LANES]
    nc = tm // t_chunk
    for t in range(t_chunk):
        for s in range(n_slabs):
            rows_t = u_scr[s, pl.ds(t, nc, stride=t_chunk), :]
            for gl in range(GROUPS_PER_SLAB):
                xg_ref[s * GROUPS_PER_SLAB + gl, :, t * SSM_GROUP:(t + 1) * SSM_GROUP] = (
                    rows_t[:, gl * SSM_GROUP:(gl + 1) * SSM_GROUP].astype(xg_ref.dtype))
    za_ref[0] = seg(1).astype(za_ref.dtype)
    pq = seg(2)
    pk = seg(3)
    pv = seg(4)
    for h in range(N_HEADS):
        k_ref[0, h] = head(pk, h)
        v_ref[0, h] = head(pv, h)
    if prompt:
        i = pl.program_id(1)
        for h in range(N_HEADS):
            for blk in range(tm // MOBA_BLOCK):
                rows = slice(blk * MOBA_BLOCK, (blk + 1) * MOBA_BLOCK)
                qt_ref[0, h, blk] = head(pq, h)[rows].T
                vt_ref[0, h, blk] = head(pv, h)[rows].T.astype(BF16)
            kb_ref[0, h] = head(pk, h).astype(BF16)
        for blk in range(tm // MOBA_BLOCK):
            km_ref[0, pl.ds(i * (tm // MOBA_BLOCK) + blk, 1), :] = jnp.mean(
                pk[blk * MOBA_BLOCK:(blk + 1) * MOBA_BLOCK], axis=0, keepdims=True)
    else:
        for h in range(N_HEADS):
            q_ref[0, h] = head(pq, h)
    zb_ref[0] = seg(5).astype(zb_ref.dtype)
    ga_ref[0] = seg(6).astype(ga_ref.dtype)
    gb_ref[0] = seg(7).astype(gb_ref.dtype)


def _in_proj(x, g, w_bf16, *, tm, t_chunk, prompt):
    nb, L, D = x.shape
    width = w_bf16.shape[1] // N_SEG
    assert width == N_HEADS * HEAD_DIM and L % tm == 0 and tm % t_chunk == 0
    G = width // SSM_GROUP
    nt = L // tm
    flat = lambda dt: jax.ShapeDtypeStruct((nb, L, width), dt)
    heads = lambda dt: jax.ShapeDtypeStruct((nb, N_HEADS, L, HEAD_DIM), dt)
    flat_spec = pl.BlockSpec((1, tm, width), lambda b, i: (b, i, 0))
    head_spec = pl.BlockSpec((1, N_HEADS, tm, HEAD_DIM), lambda b, i: (b, 0, i, 0))
    xg_shape = jax.ShapeDtypeStruct((G, nb * L // t_chunk, t_chunk * SSM_GROUP), BF16)
    xg_spec = pl.BlockSpec((G, tm // t_chunk, t_chunk * SSM_GROUP), lambda b, i: (0, b * nt + i, 0))
    if prompt:
        assert tm % MOBA_BLOCK == 0
        nblk = tm // MOBA_BLOCK
        n_blocks = L // MOBA_BLOCK
        tr = lambda dt: jax.ShapeDtypeStruct((nb, N_HEADS, n_blocks, HEAD_DIM, MOBA_BLOCK), dt)
        tr_spec = pl.BlockSpec((1, N_HEADS, nblk, HEAD_DIM, MOBA_BLOCK), lambda b, i: (b, 0, i, 0, 0))
        out_shape = [xg_shape, flat(BF16), tr(F32), heads(F32), heads(F32), heads(BF16), tr(BF16),
                     jax.ShapeDtypeStruct((nb, n_blocks, width), F32), flat(BF16), flat(BF16), flat(BF16)]
        out_specs = [xg_spec, flat_spec, tr_spec, head_spec, head_spec, head_spec, tr_spec,
                     pl.BlockSpec((1, n_blocks, width), lambda b, i: (b, 0, 0)), flat_spec, flat_spec, flat_spec]
    else:
        out_shape = [xg_shape, flat(BF16), heads(F32), heads(F32), heads(F32), flat(BF16), flat(BF16), flat(BF16)]
        out_specs = [xg_spec, flat_spec, head_spec, head_spec, head_spec, flat_spec, flat_spec, flat_spec]
    return pl.pallas_call(
        functools.partial(_in_proj_kernel, width=width, t_chunk=t_chunk, prompt=prompt),
        grid=(nb, nt),
        in_specs=[
            pl.BlockSpec((1, tm, D), lambda b, i: (b, i, 0)),
            pl.BlockSpec((1, D), lambda b, i: (0, 0)),
            pl.BlockSpec(w_bf16.shape, lambda b, i: (0, 0), pipeline_mode=pl.Buffered(1)),
        ],
        out_specs=out_specs,
        out_shape=out_shape,
        scratch_shapes=[pltpu.VMEM((width // LANES, tm, LANES), F32)],
        compiler_params=_cparams(("parallel", "arbitrary")),
        name="in_proj",
    )(x, g.reshape(1, D), w_bf16)


def _s5_weights(lam_re, lam_im, log_dt, b_re, b_im, c_re, c_im, d_skip, T):
    G, P = lam_re.shape
    H = SSM_GROUP
    dt = jnp.exp(log_dt.astype(F32))[:, None]
    lr = lam_re.astype(F32)
    li = lam_im.astype(F32)
    mag = jnp.exp(lr * dt)
    a_re = mag * jnp.cos(li * dt)
    a_im = mag * jnp.sin(li * dt)
    den = lr * lr + li * li
    coef_re = ((a_re - 1.0) * lr + a_im * li) / den
    coef_im = (a_im * lr - (a_re - 1.0) * li) / den
    br = b_re.astype(F32)
    bi = b_im.astype(F32)
    bbar_re = coef_re[..., None] * br - coef_im[..., None] * bi
    bbar_im = coef_re[..., None] * bi + coef_im[..., None] * br
    j = jnp.arange(T + 1, dtype=F32)[:, None, None]
    pm = jnp.exp(lr * dt * j)
    pr = pm * jnp.cos(li * dt * j)
    pi = pm * jnp.sin(li * dt * j)
    apr = pr[..., None] * bbar_re - pi[..., None] * bbar_im
    api = pr[..., None] * bbar_im + pi[..., None] * bbar_re
    cr = c_re.astype(F32)
    ci = c_im.astype(F32)
    kj = (jnp.einsum('ghp,jgpk->jghk', cr, apr[:T], precision=HIGHEST)
          - jnp.einsum('ghp,jgpk->jghk', ci, api[:T], precision=HIGHEST))
    lag = jnp.arange(T)[None, :] - jnp.arange(T)[:, None]
    shift = (lag[None] == jnp.arange(T)[:, None, None]).astype(F32)
    toep = jnp.einsum('jst,jghk->gskth', shift, kj, precision=HIGHEST)
    eye_t = jnp.eye(T, dtype=F32)
    eye_h = jnp.eye(H, dtype=F32)
    dmat = d_skip.astype(F32).reshape(G, H)
    toep = toep + (eye_t[None, :, None, :, None] * eye_h[None, None, :, None, :]
                   * dmat[:, None, None, None, :])
    toep = toep.reshape(G, T * H, T * H)
    wr = apr[:T][::-1].transpose(1, 0, 3, 2)
    wi = api[:T][::-1].transpose(1, 0, 3, 2)
    TH = T * H
    wr = wr.reshape(G // 2, 2, TH, P)
    wi = wi.reshape(G // 2, 2, TH, P)
    z = jnp.zeros_like(wr[:, 0])
    wst = jnp.stack([
        jnp.concatenate([wr[:, 0], z, wi[:, 0], z], axis=-1),
        jnp.concatenate([z, wr[:, 1], z, wi[:, 1]], axis=-1)], axis=1)
    p1r = pr[1:]
    p1i = pi[1:]
    car = cr[None] * p1r[:, :, None, :] - ci[None] * p1i[:, :, None, :]
    cai = cr[None] * p1i[:, :, None, :] + ci[None] * p1r[:, :, None, :]
    from_re = car.transpose(1, 3, 0, 2).reshape(G // 2, 2, P, TH)
    from_im = (-cai).transpose(1, 3, 0, 2).reshape(G // 2, 2, P, TH)
    zc = jnp.zeros_like(from_re[:, 0])
    cst = jnp.concatenate([
        jnp.concatenate([from_re[:, 0], zc], axis=-1),
        jnp.concatenate([zc, from_re[:, 1]], axis=-1),
        jnp.concatenate([from_im[:, 0], zc], axis=-1),
        jnp.concatenate([zc, from_im[:, 1]], axis=-1)], axis=1)
    at = jnp.stack([pr[T].reshape(G // 2, 2 * P), pi[T].reshape(G // 2, 2 * P)], axis=1)
    return toep.astype(BF16), wst.astype(BF16), cst.astype(BF16), at


def _s5_kernel(x_ref, toep_ref, wst_ref, cst_ref, at_ref, h0r_ref, h0i_ref,
               y_ref, hr_ref, hi_ref, s_scr, hst_scr, *, n_chunks, nb, th):
    x0 = x_ref[0]
    x1 = x_ref[1]
    sw = 2 * SSM_STATE
    s = (jnp.dot(x0, wst_ref[0, 0], preferred_element_type=F32)
         + jnp.dot(x1, wst_ref[0, 1], preferred_element_type=F32))
    s_scr[0] = s[:, :sw]
    s_scr[1] = s[:, sw:]
    ar = jnp.broadcast_to(at_ref[0, 0:1, :], (nb, sw))
    ai = jnp.broadcast_to(at_ref[0, 1:2, :], (nb, sw))

    def chunk_step(c, carry):
        hr, hi = carry
        rows = pl.ds(c, nb, stride=n_chunks) if n_chunks > 1 else pl.ds(0, nb)
        hst_scr[0, rows, :] = hr
        hst_scr[1, rows, :] = hi
        sr = s_scr[0, rows, :]
        si = s_scr[1, rows, :]
        return ar * hr - ai * hi + sr, ar * hi + ai * hr + si

    hr, hi = lax.fori_loop(0, n_chunks, chunk_step, (h0r_ref[0], h0i_ref[0]), unroll=min(n_chunks, 8))
    hr_ref[0] = hr
    hi_ref[0] = hi
    hst = jnp.concatenate([hst_scr[0], hst_scr[1]], axis=1).astype(BF16)
    ys = jnp.dot(hst, cst_ref[0], preferred_element_type=F32)
    y0 = jnp.dot(x0, toep_ref[0], preferred_element_type=F32)
    y1 = jnp.dot(x1, toep_ref[1], preferred_element_type=F32)
    y_ref[0] = (y0 + ys[:, :th]).astype(y_ref.dtype)
    y_ref[1] = (y1 + ys[:, th:]).astype(y_ref.dtype)


def _s5_mix(xg, weights, h0_re, h0_im, T):
    toep, wst, cst, at = weights
    G, rows, th = xg.shape
    nb = h0_re.shape[0]
    P = SSM_STATE
    np_ = G // 2
    C = rows // nb
    assert th == T * SSM_GROUP and rows == nb * C
    h0r = h0_re.reshape(nb, np_, 2 * P).transpose(1, 0, 2)
    h0i = h0_im.reshape(nb, np_, 2 * P).transpose(1, 0, 2)
    y, hr, hi = pl.pallas_call(
        functools.partial(_s5_kernel, n_chunks=C, nb=nb, th=th),
        grid=(np_,),
        in_specs=[
            pl.BlockSpec((2, rows, th), lambda p: (p, 0, 0)),
            pl.BlockSpec((2, th, th), lambda p: (p, 0, 0)),
            pl.BlockSpec((1, 2, th, 4 * P), lambda p: (p, 0, 0, 0)),
            pl.BlockSpec((1, 4 * P, 2 * th), lambda p: (p, 0, 0)),
            pl.BlockSpec((1, 2, 2 * P), lambda p: (p, 0, 0)),
            pl.BlockSpec((1, nb, 2 * P), lambda p: (p, 0, 0)),
            pl.BlockSpec((1, nb, 2 * P), lambda p: (p, 0, 0)),
        ],
        out_specs=[
            pl.BlockSpec((2, rows, th), lambda p: (p, 0, 0)),
            pl.BlockSpec((1, nb, 2 * P), lambda p: (p, 0, 0)),
            pl.BlockSpec((1, nb, 2 * P), lambda p: (p, 0, 0)),
        ],
        out_shape=[
            jax.ShapeDtypeStruct((G, rows, th), BF16),
            jax.ShapeDtypeStruct((np_, nb, 2 * P), F32),
            jax.ShapeDtypeStruct((np_, nb, 2 * P), F32),
        ],
        scratch_shapes=[pltpu.VMEM((2, rows, 2 * P), F32), pltpu.VMEM((2, rows, 2 * P), F32)],
        compiler_params=_cparams(("parallel",)),
        name="s5_mix",
    )(xg, toep, wst, cst, at, h0r, h0i)
    hr = hr.transpose(1, 0, 2).reshape(nb, G, P)
    hi = hi.transpose(1, 0, 2).reshape(nb, G, P)
    return y, hr, hi


AUG_ROWS = 16
SLOTS = 4


def _moba_prompt_kernel(qt_ref, qn_ref, kme_ref, kb_ref, vt_ref, augc_ref, sl_ref, o_ref,
                        mask_scr, aug_scr, sa_scr, sb_scr, *, n_blocks):
    i = pl.program_id(2)
    bs = MOBA_BLOCK
    cur = i % 2
    qs = (qt_ref[0, 0, 0] * (HEAD_DIM ** -0.5 * LOG2E)).astype(BF16)
    augc = augc_ref[0]
    slope_blk = sl_ref[0][:, :1]
    zpad = jnp.zeros((HEAD_DIM - AUG_ROWS, bs), BF16)
    r16 = lax.broadcasted_iota(jnp.int32, (AUG_ROWS, bs), 0)
    ones_rows = jnp.where((r16 >= SLOTS) & (r16 < SLOTS + 3), 1.0, 0.0)

    def scores(kb_rows, aug_tile, n_slots):
        rhs = jnp.concatenate([qs, aug_tile.astype(BF16), zpad], axis=0)
        lhs = jnp.concatenate([kb_rows, augc[:n_slots * bs]], axis=1)
        return jnp.dot(lhs, rhs, preferred_element_type=F32)

    n_trips_max = n_blocks // SLOTS

    @pl.when(i == 0)
    def _():
        for t in range(n_trips_max):
            aug_scr[cur, t] = ones_rows

    def past_scores(t, dst_ref):
        tc = jnp.minimum(t, n_trips_max - 1)
        r0 = pl.multiple_of(tc * (SLOTS * bs), SLOTS * bs)
        dst_ref[...] = scores(kb_ref[0, 0, pl.ds(r0, SLOTS * bs), :], aug_scr[cur, tc], SLOTS)

    past_scores(0, sa_scr)
    s_own = scores(kb_ref[0, 0, pl.ds(pl.multiple_of(i * bs, bs), bs), :], ones_rows, 1)

    i_next = i + 1
    means = kme_ref[0]
    st = jnp.dot(means, qn_ref[0, 0, 0], precision=HIGHEST, preferred_element_type=F32)
    jrow = lax.broadcasted_iota(jnp.int32, st.shape, 0)
    rank = jnp.zeros(st.shape, jnp.int32)
    for jp in range(n_blocks - 1):
        row = st[jp:jp + 1, :]
        beats = ((row > st) | ((row == st) & (jp < jrow))) & (jp < i_next)
        rank = rank + beats.astype(jnp.int32)
    chosen = (rank < MOBA_TOPK) & (jrow < i_next)
    mask_scr[...] = jnp.where(chosen, 0.0, NEG_INF)
    for t in range(n_trips_max):
        aug_scr[1 - cur, t] = ones_rows
        aug_scr[1 - cur, t, 0:SLOTS, :] = mask_scr[t * SLOTS:(t + 1) * SLOTS, :]

    def attend(s2, blk, beta, carry):
        m, l, acc = carry
        m_new = jnp.maximum(m, jnp.max(s2, axis=0, keepdims=True) + beta)
        alpha = jnp.exp2(m - m_new)
        p = jnp.exp2(s2 - (m_new - beta))
        l = alpha * l + jnp.sum(p, axis=0, keepdims=True)
        acc = alpha * acc + jnp.dot(vt_ref[0, 0, blk], p.astype(BF16), preferred_element_type=F32)
        return m_new, l, acc

    def past_trip(t, s_ref, carry):
        for u in range(SLOTS):
            blk = t * SLOTS + u
            beta = slope_blk * (blk - i).astype(F32)
            carry = attend(s_ref[u * bs:(u + 1) * bs, :], blk, beta, carry)
        return carry

    keys = lax.broadcasted_iota(jnp.int32, (bs, bs), 0)
    queries = lax.broadcasted_iota(jnp.int32, (bs, bs), 1)
    init = (jnp.full((1, bs), NEG_INF, F32), jnp.zeros((1, bs), F32), jnp.zeros((HEAD_DIM, bs), F32))
    carry = attend(jnp.where(keys <= queries, s_own, NEG_INF), i, 0.0, init)
    n_trips = (i + SLOTS - 1) // SLOTS

    def trip_pair(k, c):
        t0 = 2 * k
        past_scores(t0 + 1, sb_scr)
        c = past_trip(t0, sa_scr, c)
        past_scores(t0 + 2, sa_scr)
        return past_trip(t0 + 1, sb_scr, c)

    carry = lax.fori_loop(0, n_trips // 2, trip_pair, carry)
    m, l, acc = lax.cond(n_trips % 2 == 1, lambda c: past_trip(n_trips - 1, sa_scr, c), lambda c: c, carry)
    o_ref[0] = (acc / l).T.astype(o_ref.dtype)


def _moba_prompt(qt, kmeans, kb, vt, slopes):
    nb, H, n_blocks, Dh, bs = qt.shape
    L = n_blocks * bs
    assert n_blocks % SLOTS == 0 and n_blocks <= AUG_ROWS and SLOTS + 3 <= AUG_ROWS
    c = jnp.arange(bs, dtype=F32)
    bias = slopes[:, None] * c[None, :] * LOG2E
    pieces = jnp.stack(_split3_bf16(bias), axis=-1)
    onehot = jnp.eye(SLOTS, dtype=BF16)
    augc = jnp.concatenate([
        jnp.broadcast_to(onehot[None, :, None, :], (H, SLOTS, bs, SLOTS)),
        jnp.broadcast_to(pieces[:, None], (H, SLOTS, bs, 3)),
        jnp.zeros((H, SLOTS, bs, Dh - SLOTS - 3), BF16)], axis=-1).reshape(H, SLOTS * bs, Dh)
    sl = jnp.broadcast_to((slopes * (bs * LOG2E))[:, None, None], (H, 1, LANES))
    return pl.pallas_call(
        functools.partial(_moba_prompt_kernel, n_blocks=n_blocks),
        grid=(nb, H, n_blocks),
        in_specs=[
            pl.BlockSpec((1, 1, 1, Dh, bs), lambda b, h, i: (b, h, i, 0, 0)),
            pl.BlockSpec((1, 1, 1, Dh, bs), lambda b, h, i: (b, h, jnp.minimum(i + 1, n_blocks - 1), 0, 0)),
            pl.BlockSpec((1, n_blocks, Dh), lambda b, h, i: (b, 0, h)),
            pl.BlockSpec((1, 1, L, Dh), lambda b, h, i: (b, h, 0, 0)),
            pl.BlockSpec((1, 1, n_blocks, Dh, bs), lambda b, h, i: (b, h, 0, 0, 0)),
            pl.BlockSpec((1, SLOTS * bs, Dh), lambda b, h, i: (h, 0, 0)),
            pl.BlockSpec((1, 1, LANES), lambda b, h, i: (h, 0, 0)),
        ],
        out_specs=pl.BlockSpec((1, bs, Dh), lambda b, h, i: (b, i, h)),
        out_shape=jax.ShapeDtypeStruct((nb, L, H * Dh), BF16),
        scratch_shapes=[pltpu.VMEM((n_blocks, bs), F32), pltpu.VMEM((2, n_blocks // SLOTS, AUG_ROWS, bs), F32),
                        pltpu.VMEM((SLOTS * bs, bs), F32), pltpu.VMEM((SLOTS * bs, bs), F32)],
        compiler_params=_cparams(("parallel", "parallel", "arbitrary")),
        name="moba_prompt",
    )(qt, qt, kmeans, kb, vt, augc, sl)


K_PAGES_PER_STEP = 16
V_PAGES_PER_STEP = 8
PAGES_PER_BLOCK = MOBA_BLOCK // PAGE_SIZE


def _page_specs(n_pages, pps):
    def spec(e):
        return pl.BlockSpec((1, 1, N_HEADS, PAGE_SIZE, HEAD_DIM),
                            lambda s, g, pt: (0, pt[s * n_pages + g * pps + e], 0, 0, 0))
    return [spec(e) for e in range(pps)]


def _block_from_pages(page_refs, e2, h):
    ppb = PAGES_PER_BLOCK
    return jnp.concatenate([page_refs[ppb * e2 + t][0, 0, h] for t in range(ppb)], axis=0)


def _topk_rank_cols(s, n_cand):
    jidx = lax.broadcasted_iota(jnp.int32, s.shape, 1)
    rank = jnp.zeros(s.shape, jnp.int32)
    for jp in range(n_cand):
        col = s[:, jp:jp + 1]
        beats = (col > s) | ((col == s) & (jp < jidx))
        rank = rank + beats.astype(jnp.int32)
    return rank


def _moba_sample_logits_kernel(pt_ref, qbd_ref, kn_ref, bo_ref, *refs):
    k_pages = refs[:K_PAGES_PER_STEP]
    lg_ref, sc_ref, lo_ref = refs[K_PAGES_PER_STEP:]
    blocks_per_step = K_PAGES_PER_STEP // PAGES_PER_BLOCK
    g = pl.program_id(1)
    qbd = qbd_ref[0]
    rows = qbd.shape[0]
    lane = lax.broadcasted_iota(jnp.int32, (rows, LANES), 1)

    @pl.when(g == 0)
    def _():
        sc_ref[0] = jnp.zeros((rows, LANES), F32)
        tpad = kn_ref.shape[1]
        kn = jnp.concatenate([kn_ref[0], jnp.zeros((LANES - tpad, kn_ref.shape[2]), F32)], axis=0).astype(BF16)
        lo_ref[0] = _dot_nt(qbd, kn) + bo_ref[...]

    for e2 in range(blocks_per_step):
        blk = g * blocks_per_step + e2
        kcat =jnp.concatenate([_block_from_pages(k_pages, e2, h).astype(BF16) for h in range(N_HEADS)], axis=1)
        lg = _dot_nt(qbd, kcat)
        lg_ref[0, e2] = lg
        sc_ref[0] = jnp.where(lane == blk, jnp.sum(lg, axis=1, keepdims=True), sc_ref[0])


def _moba_sample_pv_kernel(pt_ref, lg0_ref, sc0_ref, lo0_ref, lgn_ref, scn_ref, lon_ref, b0_ref, slb_ref, vn_ref,
                           *refs, n_past_blocks, heads_per_step):
    v_pages = refs[:V_PAGES_PER_STEP]
    o_ref, acc_scr, p_scr, po_scr, inv_scr = refs[V_PAGES_PER_STEP:]
    blocks_per_step = V_PAGES_PER_STEP // PAGES_PER_BLOCK
    s = pl.program_id(0)
    g = pl.program_id(1)
    T = vn_ref.shape[1]
    hrows = lambda h: slice(h * T, (h + 1) * T)
    hcols = lambda h: slice(h * HEAD_DIM, (h + 1) * HEAD_DIM)
    cur = s % 2
    nxt = 1 - cur

    def softmax_rows(get_lg, sc, l_own, r0, nrows, slot):
        rws = pl.ds(r0, nrows)
        sel = _topk_rank_cols(sc, n_past_blocks) < MOBA_TOPK
        b0 = b0_ref[rws, :]
        slb = slb_ref[rws, :]
        mx = jnp.full((nrows, MOBA_BLOCK), NEG_INF, F32)
        for b in range(n_past_blocks):
            lb = jnp.where(sel[:, b:b + 1], get_lg(b) + (b0 + slb * float(b)), NEG_INF)
            p_scr[slot, b, rws, :] = lb
            mx = jnp.maximum(mx, lb)
        m = jnp.maximum(jnp.max(l_own, axis=1, keepdims=True), jnp.max(mx, axis=1, keepdims=True))
        p_own = jnp.exp(l_own - m)
        po_scr[slot, rws, :] = p_own
        sm = jnp.zeros((nrows, MOBA_BLOCK), F32)
        for b in range(n_past_blocks):
            pb = jnp.exp(p_scr[slot, b, rws, :] - m)
            p_scr[slot, b, rws, :] = pb
            sm = sm + pb
        den = jnp.sum(p_own, axis=1, keepdims=True) + jnp.sum(sm, axis=1, keepdims=True)
        inv_scr[slot, rws, :] = jnp.broadcast_to(1.0 / den, (nrows, LANES))

    @pl.when((s == 0) & (g == 0))
    def _():
        def one_head(h, _):
            r0 = pl.multiple_of(h * T, T)
            softmax_rows(lambda b: lg0_ref[0, b, pl.ds(r0, T), :], sc0_ref[0, pl.ds(r0, T), :],
                         lo0_ref[0, pl.ds(r0, T), :], r0, T, 0)
            return 0
        lax.fori_loop(0, N_HEADS, one_head, 0)

    nr = heads_per_step * T

    @pl.when(g * heads_per_step < N_HEADS)
    def _():
        r0 = pl.multiple_of(g * nr, nr)
        softmax_rows(lambda b: lgn_ref[0, b], scn_ref[0], lon_ref[0], r0, nr, nxt)

    @pl.when(g == 0)
    def _():
        pad = jnp.zeros((LANES - T, HEAD_DIM), F32)
        for h in range(N_HEADS):
            vn = jnp.concatenate([vn_ref[0, :, hcols(h)], pad], axis=0).astype(BF16)
            acc_scr[h] = jnp.dot(po_scr[cur, hrows(h), :].astype(BF16), vn, preferred_element_type=F32)

    blk0 = g * blocks_per_step
    for h in range(N_HEADS):
        vkeys = jnp.concatenate([_block_from_pages(v_pages, e2, h).astype(BF16) for e2 in range(blocks_per_step)],
                                axis=0)
        pkeys = jnp.concatenate([p_scr[cur, blk0 + e2, hrows(h), :].astype(BF16) for e2 in range(blocks_per_step)],
                                axis=1)
        acc_scr[h] += jnp.dot(pkeys, vkeys, preferred_element_type=F32)

    @pl.when(g == pl.num_programs(1) - 1)
    def _():
        for h in range(N_HEADS):
            o_ref[0, :, hcols(h)] = (acc_scr[h] * inv_scr[cur, hrows(h), :]).astype(o_ref.dtype)


def _moba_sample(q, k_new, v_new, cache_k, cache_v, page_table, slopes, past_len):
    ndb, H, T, Dh = q.shape
    n_pages = page_table.shape[1]
    assert past_len % MOBA_BLOCK == 0 and n_pages * PAGE_SIZE == past_len
    assert n_pages % K_PAGES_PER_STEP == 0 and n_pages % V_PAGES_PER_STEP == 0
    n_past_blocks = past_len // MOBA_BLOCK
    assert n_past_blocks <= LANES and T <= LANES
    k_steps = n_pages // K_PAGES_PER_STEP
    n_steps = n_pages // V_PAGES_PER_STEP
    bs = MOBA_BLOCK
    rows = H * T
    pt = page_table.reshape(-1).astype(jnp.int32)
    qbd = ((q * (Dh ** -0.5))[:, :, :, None, :] * jnp.eye(H, dtype=F32)[None, :, None, :, None])
    qbd = qbd.reshape(ndb, rows, H * Dh).astype(BF16)
    slope_row = jnp.repeat(slopes, T)[:, None]
    t_row = jnp.tile(jnp.arange(T, dtype=F32), H)[:, None]
    c = jnp.arange(bs, dtype=F32)[None, :]
    b0 = -slope_row * (past_len + t_row - c)
    slb = jnp.broadcast_to(slope_row * bs, (rows, bs))
    tc = jnp.arange(LANES, dtype=F32)[None, :]
    bo = jnp.where(tc <= t_row, -slope_row * (t_row - tc), NEG_INF)
    const = lambda shape: pl.BlockSpec(shape, lambda s, g, pt: (0,) * len(shape))
    logits, scores, l_own = pl.pallas_call(
        _moba_sample_logits_kernel,
        grid_spec=pltpu.PrefetchScalarGridSpec(
            num_scalar_prefetch=1,
            grid=(ndb, k_steps),
            in_specs=[
                pl.BlockSpec((1, rows, H * Dh), lambda s, g, pt: (s, 0, 0)),
                pl.BlockSpec((1, T, H * Dh), lambda s, g, pt: (s, 0, 0)),
                const((rows, LANES)),
            ] + _page_specs(n_pages, K_PAGES_PER_STEP),
            out_specs=[
                pl.BlockSpec((1, K_PAGES_PER_STEP // PAGES_PER_BLOCK, rows, bs), lambda s, g, pt: (s, g, 0, 0)),
                pl.BlockSpec((1, rows, LANES), lambda s, g, pt: (s, 0, 0)),
                pl.BlockSpec((1, rows, LANES), lambda s, g, pt: (s, 0, 0)),
            ],
        ),
        out_shape=[
            jax.ShapeDtypeStruct((ndb, n_past_blocks, rows, bs), F32),
            jax.ShapeDtypeStruct((ndb, rows, LANES), F32),
            jax.ShapeDtypeStruct((ndb, rows, LANES), F32),
        ],
        compiler_params=_cparams(("parallel", "arbitrary")),
        name="moba_sample_logits",
    )(pt, qbd, k_new, bo, *([cache_k] * K_PAGES_PER_STEP))
    hps = -(-H // n_steps)
    assert H % hps == 0
    n_soft = H // hps
    nr = hps * T
    nxt = lambda s: jnp.minimum(s + 1, ndb - 1)
    part = lambda g: jnp.minimum(g, n_soft - 1)
    return pl.pallas_call(
        functools.partial(_moba_sample_pv_kernel, n_past_blocks=n_past_blocks, heads_per_step=hps),
        grid_spec=pltpu.PrefetchScalarGridSpec(
            num_scalar_prefetch=1,
            grid=(ndb, n_steps),
            in_specs=[
                pl.BlockSpec((1, n_past_blocks, rows, bs), lambda s, g, pt: (0, 0, 0, 0)),
                pl.BlockSpec((1, rows, LANES), lambda s, g, pt: (0, 0, 0)),
                pl.BlockSpec((1, rows, LANES), lambda s, g, pt: (0, 0, 0)),
                pl.BlockSpec((1, n_past_blocks, nr, bs), lambda s, g, pt: (nxt(s), 0, part(g), 0)),
                pl.BlockSpec((1, nr, LANES), lambda s, g, pt: (nxt(s), part(g), 0)),
                pl.BlockSpec((1, nr, LANES), lambda s, g, pt: (nxt(s), part(g), 0)),
                const((rows, bs)), const((rows, bs)),
                pl.BlockSpec((1, T, H * Dh), lambda s, g, pt: (s, 0, 0)),
            ] + _page_specs(n_pages, V_PAGES_PER_STEP),
            out_specs=pl.BlockSpec((1, T, H * Dh), lambda s, g, pt: (s, 0, 0)),
            scratch_shapes=[pltpu.VMEM((H, T, Dh), F32), pltpu.VMEM((2, n_past_blocks, rows, bs), F32),
                            pltpu.VMEM((2, rows, LANES), F32), pltpu.VMEM((2, rows, LANES), F32)],
        ),
        out_shape=jax.ShapeDtypeStruct((ndb, T, H * Dh), BF16),
        compiler_params=_cparams(("arbitrary", "arbitrary")),
        name="moba_sample_pv",
    )(pt, logits, scores, l_own, logits, scores, l_own, b0, slb, v_new, *([cache_v] * V_PAGES_PER_STEP))


def _merge_kernel(x_ref, yg_ref, za_ref, yb_ref, zb_ref, ga_ref, gb_ref,
                  wglu_ref, wbs_ref, wba_ref, wout_ref, gf_ref, o_ref, ya_scr, *, t_chunk):
    n_slabs = ya_scr.shape[0]
    nc = yg_ref.shape[1]
    for t in range(t_chunk):
        for s in range(n_slabs):
            rows_t = jnp.concatenate(
                [yg_ref[s * GROUPS_PER_SLAB + gl, :, t * SSM_GROUP:(t + 1) * SSM_GROUP].astype(F32)
                 for gl in range(GROUPS_PER_SLAB)], axis=1)
            ya_scr[s, pl.ds(t, nc, stride=t_chunk), :] = rows_t
    ya = jnp.concatenate([ya_scr[s] for s in range(n_slabs)], axis=1)
    s = _gelu_tanh(ya).astype(BF16)
    glu = jnp.dot(s, wglu_ref[...], preferred_element_type=F32).astype(BF16)
    s = s * _sigmoid(glu) * _silu(za_ref[...])
    a = yb_ref[...] * _silu(zb_ref[...])
    mixed = (_sigmoid(ga_ref[...]) * jnp.dot(s, wbs_ref[...], preferred_element_type=F32).astype(BF16)
             + _sigmoid(gb_ref[...]) * jnp.dot(a, wba_ref[...], preferred_element_type=F32).astype(BF16))
    o = x_ref[...] + jnp.dot(mixed, wout_ref[...], preferred_element_type=F32)
    ms = jnp.mean(o * o, axis=-1, keepdims=True)
    o_ref[...] = o * lax.rsqrt(ms + NORM_EPS) * gf_ref[...]


def _merge(x, yg, za, yb, zb, ga, gb, wglu, wbs, wba, wout, gf, *, tm, t_chunk):
    n, D = x.shape
    G, _, th = yg.shape
    assert n % tm == 0 and tm % t_chunk == 0 and th == t_chunk * SSM_GROUP
    row = pl.BlockSpec((tm, D), lambda i: (i, 0))
    wspec = pl.BlockSpec((D, D), lambda i: (0, 0), pipeline_mode=pl.Buffered(1))
    return pl.pallas_call(
        functools.partial(_merge_kernel, t_chunk=t_chunk),
        grid=(n // tm,),
        in_specs=[row, pl.BlockSpec((G, tm // t_chunk, th), lambda i: (0, i, 0))] + [row] * 5 + [wspec] * 4
        + [pl.BlockSpec((1, D), lambda i: (0, 0))],
        out_specs=row,
        out_shape=jax.ShapeDtypeStruct((n, D), F32),
        scratch_shapes=[pltpu.VMEM((D // LANES, tm, LANES), F32)],
        compiler_params=_cparams(("parallel",)),
        name="merge",
    )(x, yg, za, yb, zb, ga, gb, wglu, wbs, wba, wout, gf.reshape(1, D))


def kernel(x_prompt, x_sample, cache_k, cache_v, page_table, state_ssm_re, state_ssm_im, norm_g, w_in,
           ssm_lambda_re, ssm_lambda_im, ssm_log_dt, ssm_b_re, ssm_b_im, ssm_c_re, ssm_c_im, ssm_d,
           w_glu, w_branch_ssm, w_branch_att, w_out, final_norm_g):
    depth = w_in.shape[0]
    assert depth == 1, "single-layer trunk"
    B, L, D = x_prompt.shape
    DB, T, _ = x_sample.shape
    past_len = page_table.shape[1] * PAGE_SIZE
    slopes = 2.0 ** (-8.0 * jnp.arange(1, N_HEADS + 1, dtype=F32) / N_HEADS)
    l = 0
    w_in_b = w_in[l].astype(BF16)
    wglu, wbs, wba, wout = (w[l].astype(BF16) for w in (w_glu, w_branch_ssm, w_branch_att, w_out))
    s5_params = (ssm_lambda_re[l], ssm_lambda_im[l], ssm_log_dt[l], ssm_b_re[l], ssm_b_im[l],
                 ssm_c_re[l], ssm_c_im[l], ssm_d[l])
    G = ssm_lambda_re.shape[1]

    t_chunk = 16
    xg, za, qt, k, v, kb, vt, km, zb, ga, gb = _in_proj(x_prompt, norm_g[l], w_in_b, tm=256, t_chunk=t_chunk,
                                                        prompt=True)
    zeros = jnp.zeros((B, G, SSM_STATE), F32)
    yg, hr_p, hi_p = _s5_mix(xg, _s5_weights(*s5_params, t_chunk), zeros, zeros, t_chunk)
    yb = _moba_prompt(qt, km, kb, vt, slopes)
    n = B * L
    r2 = lambda t: t.reshape(n, D)
    y_prompt = _merge(r2(x_prompt), yg, r2(za), r2(yb), r2(zb), r2(ga), r2(gb),
                      wglu, wbs, wba, wout, final_norm_g, tm=512, t_chunk=t_chunk).reshape(B, L, D)

    ns = DB * T
    xgs, zas, qs, ks, vs, zbs, gas, gbs = _in_proj(x_sample.reshape(1, ns, D), norm_g[l], w_in_b, tm=256,
                                                   t_chunk=T, prompt=False)
    heads = lambda t: t.reshape(N_HEADS, DB, T, HEAD_DIM).transpose(1, 0, 2, 3)
    qs, ks, vs = heads(qs), heads(ks), heads(vs)
    nat = lambda t: t.transpose(0, 2, 1, 3).reshape(DB, T, N_HEADS * HEAD_DIM)
    ygs, hr_s, hi_s = _s5_mix(xgs, _s5_weights(*s5_params, T), state_ssm_re[l], state_ssm_im[l], T)
    ybs = _moba_sample(qs, nat(ks), nat(vs), cache_k, cache_v, page_table, slopes, past_len)
    r2s = lambda t: t.reshape(ns, D)
    y_sample = _merge(r2s(x_sample), ygs, r2s(zas), r2s(ybs), r2s(zbs), r2s(gas), r2s(gbs),
                      wglu, wbs, wba, wout, final_norm_g, tm=256, t_chunk=T).reshape(DB, T, D)

    return (y_prompt, y_sample, k[None], v[None], hr_p[None], hi_p[None], ks[None], vs[None], hr_s[None], hi_s[None])
```

```python
import functools
import math

import jax
import jax.numpy as jnp
from jax import lax
from jax.experimental import pallas as pl
from jax.experimental.pallas import tpu as pltpu

F32 = jnp.float32
BF16 = jnp.bfloat16

N_HEADS = 8
HEAD_DIM = 128
SSM_GROUP = 16
SSM_STATE = 64
MOBA_BLOCK = 256
MOBA_TOPK = 3
PAGE_SIZE = 128
NORM_EPS = 1e-6
NEG_INF = -1e30
N_SEG = 8
LANES = 128
GROUPS_PER_SLAB = LANES // SSM_GROUP
LOG2E = math.log2(math.e)

VMEM_LIMIT = 56 * 1024 * 1024
HIGHEST = lax.Precision.HIGHEST


def _cparams(sem):
    return pltpu.CompilerParams(dimension_semantics=sem, vmem_limit_bytes=VMEM_LIMIT)


def _sigmoid(x):
    return 0.5 * jnp.tanh(0.5 * x) + 0.5


def _silu(x):
    return x * _sigmoid(x)


def _gelu_tanh(x):
    c = math.sqrt(2.0 / math.pi)
    return 0.5 * x * (1.0 + jnp.tanh(c * (x + 0.044715 * (x * x * x))))


def _dot_nt(a, b, **kw):
    return lax.dot_general(a, b, (((1,), (1,)), ((), ())), preferred_element_type=F32, **kw)


def _bf16_trunc(x):
    bits = lax.bitcast_convert_type(x, jnp.uint32) & jnp.uint32(0xFFFF0000)
    return lax.bitcast_convert_type(bits, F32)


def _split3_bf16(x):
    hi = _bf16_trunc(x)
    mid = _bf16_trunc(x - hi)
    lo = x - hi - mid
    return hi.astype(BF16), mid.astype(BF16), lo.astype(BF16)


def _in_proj_kernel(x_ref, g_ref, w_ref, *refs, width, t_chunk, prompt):
    if prompt:
        xg_ref, za_ref, qt_ref, k_ref, v_ref, kb_ref, vt_ref, km_ref, zb_ref, ga_ref, gb_ref, u_scr = refs
    else:
        xg_ref, za_ref, q_ref, k_ref, v_ref, zb_ref, ga_ref, gb_ref, u_scr = refs
    x = x_ref[0]
    tm = x.shape[0]
    ms = jnp.mean(x * x, axis=-1, keepdims=True)
    xn = (x * lax.rsqrt(ms + NORM_EPS) * g_ref[...]).astype(BF16)

    def seg(s):
        return jnp.dot(xn, w_ref[:, s * width:(s + 1) * width], preferred_element_type=F32)

    def head(p, h):
        return p[:, h * HEAD_DIM:(h + 1) * HEAD_DIM]

    pu = seg(0)
    n_slabs = width // LANES
    for s in range(n_slabs):
        u_scr[s] = pu[:, s * LANES:(s + 1) * LANES]
    nc = tm // t_chunk
    for t in range(t_chunk):
        for s in range(n_slabs):
            rows_t = u_scr[s, pl.ds(t, nc, stride=t_chunk), :]
            for gl in range(GROUPS_PER_SLAB):
                xg_ref[s * GROUPS_PER_SLAB + gl, :, t * SSM_GROUP:(t + 1) * SSM_GROUP] = (
                    rows_t[:, gl * SSM_GROUP:(gl + 1) * SSM_GROUP].astype(xg_ref.dtype))
    za_ref[0] = seg(1).astype(za_ref.dtype)
    pq = seg(2)
    pk = seg(3)
    pv = seg(4)
    for h in range(N_HEADS):
        k_ref[0, h] = head(pk, h)
        v_ref[0, h] = head(pv, h)
    if prompt:
        i = pl.program_id(1)
        for h in range(N_HEADS):
            for blk in range(tm // MOBA_BLOCK):
                rows = slice(blk * MOBA_BLOCK, (blk + 1) * MOBA_BLOCK)
                qt_ref[0, h, blk] = head(pq, h)[rows].T
                vt_ref[0, h, blk] = head(pv, h)[rows].T.astype(BF16)
            kb_ref[0, h] = head(pk, h).astype(BF16)
        for blk in range(tm // MOBA_BLOCK):
            km_ref[0, pl.ds(i * (tm // MOBA_BLOCK) + blk, 1), :] = jnp.mean(
                pk[blk * MOBA_BLOCK:(blk + 1) * MOBA_BLOCK], axis=0, keepdims=True)
    else:
        for h in range(N_HEADS):
            q_ref[0, h] = head(pq, h)
    zb_ref[0] = seg(5).astype(zb_ref.dtype)
    ga_ref[0] = seg(6).astype(ga_ref.dtype)
    gb_ref[0] = seg(7).astype(gb_ref.dtype)


def _in_proj(x, g, w_bf16, *, tm, t_chunk, prompt):
    nb, L, D = x.shape
    width = w_bf16.shape[1] // N_SEG
    assert width == N_HEADS * HEAD_DIM and L % tm == 0 and tm % t_chunk == 0
    G = width // SSM_GROUP
    nt = L // tm
    flat = lambda dt: jax.ShapeDtypeStruct((nb, L, width), dt)
    heads = lambda dt: jax.ShapeDtypeStruct((nb, N_HEADS, L, HEAD_DIM), dt)
    flat_spec = pl.BlockSpec((1, tm, width), lambda b, i: (b, i, 0))
    head_spec = pl.BlockSpec((1, N_HEADS, tm, HEAD_DIM), lambda b, i: (b, 0, i, 0))
    xg_shape = jax.ShapeDtypeStruct((G, nb * L // t_chunk, t_chunk * SSM_GROUP), BF16)
    xg_spec = pl.BlockSpec((G, tm // t_chunk, t_chunk * SSM_GROUP), lambda b, i: (0, b * nt + i, 0))
    if prompt:
        assert tm % MOBA_BLOCK == 0
        nblk = tm // MOBA_BLOCK
        n_blocks = L // MOBA_BLOCK
        tr = lambda dt: jax.ShapeDtypeStruct((nb, N_HEADS, n_blocks, HEAD_DIM, MOBA_BLOCK), dt)
        tr_spec = pl.BlockSpec((1, N_HEADS, nblk, HEAD_DIM, MOBA_BLOCK), lambda b, i: (b, 0, i, 0, 0))
        out_shape = [xg_shape, flat(BF16), tr(F32), heads(F32), heads(F32), heads(BF16), tr(BF16),
                     jax.ShapeDtypeStruct((nb, n_blocks, width), F32), flat(BF16), flat(BF16), flat(BF16)]
        out_specs = [xg_spec, flat_spec, tr_spec, head_spec, head_spec, head_spec, tr_spec,
                     pl.BlockSpec((1, n_blocks, width), lambda b, i: (b, 0, 0)), flat_spec, flat_spec, flat_spec]
    else:
        out_shape = [xg_shape, flat(BF16), heads(F32), heads(F32), heads(F32), flat(BF16), flat(BF16), flat(BF16)]
        out_specs = [xg_spec, flat_spec, head_spec, head_spec, head_spec, flat_spec, flat_spec, flat_spec]
    return pl.pallas_call(
        functools.partial(_in_proj_kernel, width=width, t_chunk=t_chunk, prompt=prompt),
        grid=(nb, nt),
        in_specs=[
            pl.BlockSpec((1, tm, D), lambda b, i: (b, i, 0)),
            pl.BlockSpec((1, D), lambda b, i: (0, 0)),
            pl.BlockSpec(w_bf16.shape, lambda b, i: (0, 0), pipeline_mode=pl.Buffered(1)),
        ],
        out_specs=out_specs,
        out_shape=out_shape,
        scratch_shapes=[pltpu.VMEM((width // LANES, tm, LANES), F32)],
        compiler_params=_cparams(("parallel", "arbitrary")),
        name="in_proj",
    )(x, g.reshape(1, D), w_bf16)


def _s5_weights(lam_re, lam_im, log_dt, b_re, b_im, c_re, c_im, d_skip, T):
    G, P = lam_re.shape
    H = SSM_GROUP
    dt = jnp.exp(log_dt.astype(F32))[:, None]
    lr = lam_re.astype(F32)
    li = lam_im.astype(F32)
    mag = jnp.exp(lr * dt)
    a_re = mag * jnp.cos(li * dt)
    a_im = mag * jnp.sin(li * dt)
    den = lr * lr + li * li
    coef_re = ((a_re - 1.0) * lr + a_im * li) / den
    coef_im = (a_im * lr - (a_re - 1.0) * li) / den
    br = b_re.astype(F32)
    bi = b_im.astype(F32)
    bbar_re = coef_re[..., None] * br - coef_im[..., None] * bi
    bbar_im = coef_re[..., None] * bi + coef_im[..., None] * br
    j = jnp.arange(T + 1, dtype=F32)[:, None, None]
    pm = jnp.exp(lr * dt * j)
    pr = pm * jnp.cos(li * dt * j)
    pi = pm * jnp.sin(li * dt * j)
    apr = pr[..., None] * bbar_re - pi[..., None] * bbar_im
    api = pr[..., None] * bbar_im + pi[..., None] * bbar_re
    cr = c_re.astype(F32)
    ci = c_im.astype(F32)
    kj = (jnp.einsum('ghp,jgpk->jghk', cr, apr[:T], precision=HIGHEST)
          - jnp.einsum('ghp,jgpk->jghk', ci, api[:T], precision=HIGHEST))
    lag = jnp.arange(T)[None, :] - jnp.arange(T)[:, None]
    shift = (lag[None] == jnp.arange(T)[:, None, None]).astype(F32)
    toep = jnp.einsum('jst,jghk->gskth', shift, kj, precision=HIGHEST)
    eye_t = jnp.eye(T, dtype=F32)
    eye_h = jnp.eye(H, dtype=F32)
    dmat = d_skip.astype(F32).reshape(G, H)
    toep = toep + (eye_t[None, :, None, :, None] * eye_h[None, None, :, None, :]
                   * dmat[:, None, None, None, :])
    toep = toep.reshape(G, T * H, T * H)
    wr = apr[:T][::-1].transpose(1, 0, 3, 2)
    wi = api[:T][::-1].transpose(1, 0, 3, 2)
    TH = T * H
    wr = wr.reshape(G // 2, 2, TH, P)
    wi = wi.reshape(G // 2, 2, TH, P)
    z = jnp.zeros_like(wr[:, 0])
    wst = jnp.stack([
        jnp.concatenate([wr[:, 0], z, wi[:, 0], z], axis=-1),
        jnp.concatenate([z, wr[:, 1], z, wi[:, 1]], axis=-1)], axis=1)
    p1r = pr[1:]
    p1i = pi[1:]
    car = cr[None] * p1r[:, :, None, :] - ci[None] * p1i[:, :, None, :]
    cai = cr[None] * p1i[:, :, None, :] + ci[None] * p1r[:, :, None, :]
    from_re = car.transpose(1, 3, 0, 2).reshape(G // 2, 2, P, TH)
    from_im = (-cai).transpose(1, 3, 0, 2).reshape(G // 2, 2, P, TH)
    zc = jnp.zeros_like(from_re[:, 0])
    cst = jnp.concatenate([
        jnp.concatenate([from_re[:, 0], zc], axis=-1),
        jnp.concatenate([zc, from_re[:, 1]], axis=-1),
        jnp.concatenate([from_im[:, 0], zc], axis=-1),
        jnp.concatenate([zc, from_im[:, 1]], axis=-1)], axis=1)
    at = jnp.stack([pr[T].reshape(G // 2, 2 * P), pi[T].reshape(G // 2, 2 * P)], axis=1)
    return toep.astype(BF16), wst.astype(BF16), cst.astype(BF16), at


def _s5_kernel(x_ref, toep_ref, wst_ref, cst_ref, at_ref, h0r_ref, h0i_ref,
               y_ref, hr_ref, hi_ref, s_scr, hst_scr, *, n_chunks, nb, th):
    x0 = x_ref[0]
    x1 = x_ref[1]
    sw = 2 * SSM_STATE
    s = (jnp.dot(x0, wst_ref[0, 0], preferred_element_type=F32)
         + jnp.dot(x1, wst_ref[0, 1], preferred_element_type=F32))
    s_scr[0] = s[:, :sw]
    s_scr[1] = s[:, sw:]
    ar = jnp.broadcast_to(at_ref[0, 0:1, :], (nb, sw))
    ai = jnp.broadcast_to(at_ref[0, 1:2, :], (nb, sw))

    def chunk_step(c, carry):
        hr, hi = carry
        rows = pl.ds(c, nb, stride=n_chunks) if n_chunks > 1 else pl.ds(0, nb)
        hst_scr[0, rows, :] = hr
        hst_scr[1, rows, :] = hi
        sr = s_scr[0, rows, :]
        si = s_scr[1, rows, :]
        return ar * hr - ai * hi + sr, ar * hi + ai * hr + si

    hr, hi = lax.fori_loop(0, n_chunks, chunk_step, (h0r_ref[0], h0i_ref[0]), unroll=min(n_chunks, 8))
    hr_ref[0] = hr
    hi_ref[0] = hi
    hst = jnp.concatenate([hst_scr[0], hst_scr[1]], axis=1).astype(BF16)
    ys = jnp.dot(hst, cst_ref[0], preferred_element_type=F32)
    y0 = jnp.dot(x0, toep_ref[0], preferred_element_type=F32)
    y1 = jnp.dot(x1, toep_ref[1], preferred_element_type=F32)
    y_ref[0] = (y0 + ys[:, :th]).astype(y_ref.dtype)
    y_ref[1] = (y1 + ys[:, th:]).astype(y_ref.dtype)


def _s5_mix(xg, weights, h0_re, h0_im, T):
    toep, wst, cst, at = weights
    G, rows, th = xg.shape
    nb = h0_re.shape[0]
    P = SSM_STATE
    np_ = G // 2
    C = rows // nb
    assert th == T * SSM_GROUP and rows == nb * C
    h0r = h0_re.reshape(nb, np_, 2 * P).transpose(1, 0, 2)
    h0i = h0_im.reshape(nb, np_, 2 * P).transpose(1, 0, 2)
    y, hr, hi = pl.pallas_call(
        functools.partial(_s5_kernel, n_chunks=C, nb=nb, th=th),
        grid=(np_,),
        in_specs=[
            pl.BlockSpec((2, rows, th), lambda p: (p, 0, 0)),
            pl.BlockSpec((2, th, th), lambda p: (p, 0, 0)),
            pl.BlockSpec((1, 2, th, 4 * P), lambda p: (p, 0, 0, 0)),
            pl.BlockSpec((1, 4 * P, 2 * th), lambda p: (p, 0, 0)),
            pl.BlockSpec((1, 2, 2 * P), lambda p: (p, 0, 0)),
            pl.BlockSpec((1, nb, 2 * P), lambda p: (p, 0, 0)),
            pl.BlockSpec((1, nb, 2 * P), lambda p: (p, 0, 0)),
        ],
        out_specs=[
            pl.BlockSpec((2, rows, th), lambda p: (p, 0, 0)),
            pl.BlockSpec((1, nb, 2 * P), lambda p: (p, 0, 0)),
            pl.BlockSpec((1, nb, 2 * P), lambda p: (p, 0, 0)),
        ],
        out_shape=[
            jax.ShapeDtypeStruct((G, rows, th), BF16),
            jax.ShapeDtypeStruct((np_, nb, 2 * P), F32),
            jax.ShapeDtypeStruct((np_, nb, 2 * P), F32),
        ],
        scratch_shapes=[pltpu.VMEM((2, rows, 2 * P), F32), pltpu.VMEM((2, rows, 2 * P), F32)],
        compiler_params=_cparams(("parallel",)),
        name="s5_mix",
    )(xg, toep, wst, cst, at, h0r, h0i)
    hr = hr.transpose(1, 0, 2).reshape(nb, G, P)
    hi = hi.transpose(1, 0, 2).reshape(nb, G, P)
    return y, hr, hi


AUG_ROWS = 16
SLOTS = 4
HEADS_PER_STEP = 4


def _moba_prompt_kernel(qt_ref, qn_ref, kme_ref, kb_ref, vt_ref, augc_ref, sl_ref, o_ref,
                        mask_scr, aug_scr, sa_scr, sb_scr, *, n_blocks):
    i = pl.program_id(2)
    bs = MOBA_BLOCK
    cur = i % 2
    heads = range(HEADS_PER_STEP)
    qs = [(qt_ref[0, hh, 0] * (HEAD_DIM ** -0.5 * LOG2E)).astype(BF16) for hh in heads]
    slope_blk = [sl_ref[hh][:, :1] for hh in heads]
    zpad = jnp.zeros((HEAD_DIM - AUG_ROWS, bs), BF16)
    r16 = lax.broadcasted_iota(jnp.int32, (AUG_ROWS, bs), 0)
    ones_rows = jnp.where((r16 >= SLOTS) & (r16 < SLOTS + 3), 1.0, 0.0)

    def scores(hh, kb_rows, aug_tile, n_slots):
        rhs = jnp.concatenate([qs[hh], aug_tile.astype(BF16), zpad], axis=0)
        lhs = jnp.concatenate([kb_rows, augc_ref[hh, 0:n_slots * bs, :]], axis=1)
        return jnp.dot(lhs, rhs, preferred_element_type=F32)

    n_trips_max = n_blocks // SLOTS

    @pl.when(i == 0)
    def _():
        for hh in heads:
            for t in range(n_trips_max):
                aug_scr[hh, cur, t] = ones_rows

    def past_scores(t, dst_ref):
        tc = jnp.minimum(t, n_trips_max - 1)
        r0 = pl.multiple_of(tc * (SLOTS * bs), SLOTS * bs)
        for hh in heads:
            dst_ref[hh] = scores(hh, kb_ref[0, hh, pl.ds(r0, SLOTS * bs), :], aug_scr[hh, cur, tc], SLOTS)

    past_scores(0, sa_scr)
    own0 = pl.multiple_of(i * bs, bs)
    s_own = [scores(hh, kb_ref[0, hh, pl.ds(own0, bs), :], ones_rows, 1) for hh in heads]

    i_next = i + 1
    for hh in heads:
        means = kme_ref[0, :, hh * HEAD_DIM:(hh + 1) * HEAD_DIM]
        st = jnp.dot(means, qn_ref[0, hh, 0], precision=HIGHEST, preferred_element_type=F32)
        jrow = lax.broadcasted_iota(jnp.int32, st.shape, 0)
        rank = jnp.zeros(st.shape, jnp.int32)
        for jp in range(n_blocks - 1):
            row = st[jp:jp + 1, :]
            beats = ((row > st) | ((row == st) & (jp < jrow))) & (jp < i_next)
            rank = rank + beats.astype(jnp.int32)
        chosen = (rank < MOBA_TOPK) & (jrow < i_next)
        mask_scr[hh] = jnp.where(chosen, 0.0, NEG_INF)
        for t in range(n_trips_max):
            aug_scr[hh, 1 - cur, t] = ones_rows
            aug_scr[hh, 1 - cur, t, 0:SLOTS, :] = mask_scr[hh, t * SLOTS:(t + 1) * SLOTS, :]

    def attend(hh, s2, blk, beta, carry):
        m, l, acc = carry
        m_new = jnp.maximum(m, jnp.max(s2, axis=0, keepdims=True) + beta)
        alpha = jnp.exp2(m - m_new)
        p = jnp.exp2(s2 - (m_new - beta))
        l = alpha * l + jnp.sum(p, axis=0, keepdims=True)
        acc = alpha * acc + jnp.dot(vt_ref[0, hh, blk], p.astype(BF16), preferred_element_type=F32)
        return m_new, l, acc

    def past_trip(t, s_ref, carries):
        carries = list(carries)
        for u in range(SLOTS):
            blk = t * SLOTS + u
            for hh in heads:
                beta = slope_blk[hh] * (blk - i).astype(F32)
                carries[hh] = attend(hh, s_ref[hh, u * bs:(u + 1) * bs, :], blk, beta, carries[hh])
        return tuple(carries)

    keys = lax.broadcasted_iota(jnp.int32, (bs, bs), 0)
    queries = lax.broadcasted_iota(jnp.int32, (bs, bs), 1)
    init = (jnp.full((1, bs), NEG_INF, F32), jnp.zeros((1, bs), F32), jnp.zeros((HEAD_DIM, bs), F32))
    carries = tuple(attend(hh, jnp.where(keys <= queries, s_own[hh], NEG_INF), i, 0.0, init) for hh in heads)
    n_trips = (i + SLOTS - 1) // SLOTS

    def trip_pair(k, c):
        t0 = 2 * k
        past_scores(t0 + 1, sb_scr)
        c = past_trip(t0, sa_scr, c)
        past_scores(t0 + 2, sa_scr)
        return past_trip(t0 + 1, sb_scr, c)

    carries = lax.fori_loop(0, n_trips // 2, trip_pair, carries)
    carries = lax.cond(n_trips % 2 == 1, lambda c: past_trip(n_trips - 1, sa_scr, c), lambda c: c, carries)
    for hh in heads:
        m, l, acc = carries[hh]
        o_ref[0, :, hh * HEAD_DIM:(hh + 1) * HEAD_DIM] = (acc / l).T.astype(o_ref.dtype)


def _moba_prompt(qt, kmeans, kb, vt, slopes):
    nb, H, n_blocks, Dh, bs = qt.shape
    L = n_blocks * bs
    assert n_blocks % SLOTS == 0 and n_blocks <= AUG_ROWS and SLOTS + 3 <= AUG_ROWS
    c = jnp.arange(bs, dtype=F32)
    bias = slopes[:, None] * c[None, :] * LOG2E
    pieces = jnp.stack(_split3_bf16(bias), axis=-1)
    onehot = jnp.eye(SLOTS, dtype=BF16)
    augc = jnp.concatenate([
        jnp.broadcast_to(onehot[None, :, None, :], (H, SLOTS, bs, SLOTS)),
        jnp.broadcast_to(pieces[:, None], (H, SLOTS, bs, 3)),
        jnp.zeros((H, SLOTS, bs, Dh - SLOTS - 3), BF16)], axis=-1).reshape(H, SLOTS * bs, Dh)
    sl = jnp.broadcast_to((slopes * (bs * LOG2E))[:, None, None], (H, 1, LANES))
    hp = HEADS_PER_STEP
    assert H % hp == 0
    return pl.pallas_call(
        functools.partial(_moba_prompt_kernel, n_blocks=n_blocks),
        grid=(nb, H // hp, n_blocks),
        in_specs=[
            pl.BlockSpec((1, hp, 1, Dh, bs), lambda b, h, i: (b, h, i, 0, 0)),
            pl.BlockSpec((1, hp, 1, Dh, bs), lambda b, h, i: (b, h, jnp.minimum(i + 1, n_blocks - 1), 0, 0)),
            pl.BlockSpec((1, n_blocks, hp * Dh), lambda b, h, i: (b, 0, h)),
            pl.BlockSpec((1, hp, L, Dh), lambda b, h, i: (b, h, 0, 0)),
            pl.BlockSpec((1, hp, n_blocks, Dh, bs), lambda b, h, i: (b, h, 0, 0, 0)),
            pl.BlockSpec((hp, SLOTS * bs, Dh), lambda b, h, i: (h, 0, 0)),
            pl.BlockSpec((hp, 1, LANES), lambda b, h, i: (h, 0, 0)),
        ],
        out_specs=pl.BlockSpec((1, bs, hp * Dh), lambda b, h, i: (b, i, h)),
        out_shape=jax.ShapeDtypeStruct((nb, L, H * Dh), BF16),
        scratch_shapes=[pltpu.VMEM((hp, n_blocks, bs), F32),
                        pltpu.VMEM((hp, 2, n_blocks // SLOTS, AUG_ROWS, bs), F32),
                        pltpu.VMEM((hp, SLOTS * bs, bs), F32), pltpu.VMEM((hp, SLOTS * bs, bs), F32)],
        compiler_params=_cparams(("parallel", "parallel", "arbitrary")),
        name="moba_prompt",
    )(qt, qt, kmeans, kb, vt, augc, sl)


K_PAGES_PER_STEP = 16
V_PAGES_PER_STEP = 8
PAGES_PER_BLOCK = MOBA_BLOCK // PAGE_SIZE


def _page_specs(n_pages, pps):
    def spec(e):
        return pl.BlockSpec((1, 1, N_HEADS, PAGE_SIZE, HEAD_DIM),
                            lambda s, g, pt: (0, pt[s * n_pages + g * pps + e], 0, 0, 0))
    return [spec(e) for e in range(pps)]


def _block_from_pages(page_refs, e2, h):
    ppb = PAGES_PER_BLOCK
    return jnp.concatenate([page_refs[ppb * e2 + t][0, 0, h] for t in range(ppb)], axis=0)


def _topk_rank_cols(s, n_cand):
    jidx = lax.broadcasted_iota(jnp.int32, s.shape, 1)
    rank = jnp.zeros(s.shape, jnp.int32)
    for jp in range(n_cand):
        col = s[:, jp:jp + 1]
        beats = (col > s) | ((col == s) & (jp < jidx))
        rank = rank + beats.astype(jnp.int32)
    return rank


def _moba_sample_logits_kernel(pt_ref, qbd_ref, kn_ref, bo_ref, *refs):
    k_pages = refs[:K_PAGES_PER_STEP]
    lg_ref, sc_ref, lo_ref = refs[K_PAGES_PER_STEP:]
    blocks_per_step = K_PAGES_PER_STEP // PAGES_PER_BLOCK
    g = pl.program_id(1)
    qbd = qbd_ref[0]
    rows = qbd.shape[0]
    lane = lax.broadcasted_iota(jnp.int32, (rows, LANES), 1)

    @pl.when(g == 0)
    def _():
        sc_ref[0] = jnp.zeros((rows, LANES), F32)
        tpad = kn_ref.shape[1]
        kn = jnp.concatenate([kn_ref[0], jnp.zeros((LANES - tpad, kn_ref.shape[2]), F32)], axis=0).astype(BF16)
        lo_ref[0] = _dot_nt(qbd, kn) + bo_ref[...]

    for e2 in range(blocks_per_step):
        blk = g * blocks_per_step + e2
        kcat =jnp.concatenate([_block_from_pages(k_pages, e2, h).astype(BF16) for h in range(N_HEADS)], axis=1)
        lg = _dot_nt(qbd, kcat)
        lg_ref[0, e2] = lg
        sc_ref[0] = jnp.where(lane == blk, jnp.sum(lg, axis=1, keepdims=True), sc_ref[0])


def _moba_sample_pv_kernel(pt_ref, lg0_ref, sc0_ref, lo0_ref, lgn_ref, scn_ref, lon_ref, b0_ref, slb_ref, vn_ref,
                           *refs, n_past_blocks, heads_per_step):
    v_pages = refs[:V_PAGES_PER_STEP]
    o_ref, acc_scr, p_scr, po_scr, inv_scr = refs[V_PAGES_PER_STEP:]
    blocks_per_step = V_PAGES_PER_STEP // PAGES_PER_BLOCK
    s = pl.program_id(0)
    g = pl.program_id(1)
    T = vn_ref.shape[1]
    hrows = lambda h: slice(h * T, (h + 1) * T)
    hcols = lambda h: slice(h * HEAD_DIM, (h + 1) * HEAD_DIM)
    cur = s % 2
    nxt = 1 - cur

    def softmax_rows(get_lg, sc, l_own, r0, nrows, slot):
        rws = pl.ds(r0, nrows)
        sel = _topk_rank_cols(sc, n_past_blocks) < MOBA_TOPK
        b0 = b0_ref[rws, :]
        slb = slb_ref[rws, :]
        mx = jnp.full((nrows, MOBA_BLOCK), NEG_INF, F32)
        for b in range(n_past_blocks):
            lb = jnp.where(sel[:, b:b + 1], get_lg(b) + (b0 + slb * float(b)), NEG_INF)
            p_scr[slot, b, rws, :] = lb
            mx = jnp.maximum(mx, lb)
        m = jnp.maximum(jnp.max(l_own, axis=1, keepdims=True), jnp.max(mx, axis=1, keepdims=True))
        p_own = jnp.exp(l_own - m)
        po_scr[slot, rws, :] = p_own
        sm = jnp.zeros((nrows, MOBA_BLOCK), F32)
        for b in range(n_past_blocks):
            pb = jnp.exp(p_scr[slot, b, rws, :] - m)
            p_scr[slot, b, rws, :] = pb
            sm = sm + pb
        den = jnp.sum(p_own, axis=1, keepdims=True) + jnp.sum(sm, axis=1, keepdims=True)
        inv_scr[slot, rws, :] = jnp.broadcast_to(1.0 / den, (nrows, LANES))

    @pl.when((s == 0) & (g == 0))
    def _():
        def one_head(h, _):
            r0 = pl.multiple_of(h * T, T)
            softmax_rows(lambda b: lg0_ref[0, b, pl.ds(r0, T), :], sc0_ref[0, pl.ds(r0, T), :],
                         lo0_ref[0, pl.ds(r0, T), :], r0, T, 0)
            return 0
        lax.fori_loop(0, N_HEADS, one_head, 0)

    nr = heads_per_step * T

    @pl.when(g * heads_per_step < N_HEADS)
    def _():
        r0 = pl.multiple_of(g * nr, nr)
        softmax_rows(lambda b: lgn_ref[0, b], scn_ref[0], lon_ref[0], r0, nr, nxt)

    @pl.when(g == 0)
    def _():
        pad = jnp.zeros((LANES - T, HEAD_DIM), F32)
        for h in range(N_HEADS):
            vn = jnp.concatenate([vn_ref[0, :, hcols(h)], pad], axis=0).astype(BF16)
            acc_scr[h] = jnp.dot(po_scr[cur, hrows(h), :].astype(BF16), vn, preferred_element_type=F32)

    blk0 = g * blocks_per_step
    for h in range(N_HEADS):
        vkeys = jnp.concatenate([_block_from_pages(v_pages, e2, h).astype(BF16) for e2 in range(blocks_per_step)],
                                axis=0)
        pkeys = jnp.concatenate([p_scr[cur, blk0 + e2, hrows(h), :].astype(BF16) for e2 in range(blocks_per_step)],
                                axis=1)
        acc_scr[h] += jnp.dot(pkeys, vkeys, preferred_element_type=F32)

    @pl.when(g == pl.num_programs(1) - 1)
    def _():
        for h in range(N_HEADS):
            o_ref[0, :, hcols(h)] = (acc_scr[h] * inv_scr[cur, hrows(h), :]).astype(o_ref.dtype)


def _moba_sample(q, k_new, v_new, cache_k, cache_v, page_table, slopes, past_len):
    ndb, H, T, Dh = q.shape
    n_pages = page_table.shape[1]
    assert past_len % MOBA_BLOCK == 0 and n_pages * PAGE_SIZE == past_len
    assert n_pages % K_PAGES_PER_STEP == 0 and n_pages % V_PAGES_PER_STEP == 0
    n_past_blocks = past_len // MOBA_BLOCK
    assert n_past_blocks <= LANES and T <= LANES
    k_steps = n_pages // K_PAGES_PER_STEP
    n_steps = n_pages // V_PAGES_PER_STEP
    bs = MOBA_BLOCK
    rows = H * T
    pt = page_table.reshape(-1).astype(jnp.int32)
    qbd = ((q * (Dh ** -0.5))[:, :, :, None, :] * jnp.eye(H, dtype=F32)[None, :, None, :, None])
    qbd = qbd.reshape(ndb, rows, H * Dh).astype(BF16)
    slope_row = jnp.repeat(slopes, T)[:, None]
    t_row = jnp.tile(jnp.arange(T, dtype=F32), H)[:, None]
    c = jnp.arange(bs, dtype=F32)[None, :]
    b0 = -slope_row * (past_len + t_row - c)
    slb = jnp.broadcast_to(slope_row * bs, (rows, bs))
    tc = jnp.arange(LANES, dtype=F32)[None, :]
    bo = jnp.where(tc <= t_row, -slope_row * (t_row - tc), NEG_INF)
    const = lambda shape: pl.BlockSpec(shape, lambda s, g, pt: (0,) * len(shape))
    logits, scores, l_own = pl.pallas_call(
        _moba_sample_logits_kernel,
        grid_spec=pltpu.PrefetchScalarGridSpec(
            num_scalar_prefetch=1,
            grid=(ndb, k_steps),
            in_specs=[
                pl.BlockSpec((1, rows, H * Dh), lambda s, g, pt: (s, 0, 0)),
                pl.BlockSpec((1, T, H * Dh), lambda s, g, pt: (s, 0, 0)),
                const((rows, LANES)),
            ] + _page_specs(n_pages, K_PAGES_PER_STEP),
            out_specs=[
                pl.BlockSpec((1, K_PAGES_PER_STEP // PAGES_PER_BLOCK, rows, bs), lambda s, g, pt: (s, g, 0, 0)),
                pl.BlockSpec((1, rows, LANES), lambda s, g, pt: (s, 0, 0)),
                pl.BlockSpec((1, rows, LANES), lambda s, g, pt: (s, 0, 0)),
            ],
        ),
        out_shape=[
            jax.ShapeDtypeStruct((ndb, n_past_blocks, rows, bs), F32),
            jax.ShapeDtypeStruct((ndb, rows, LANES), F32),
            jax.ShapeDtypeStruct((ndb, rows, LANES), F32),
        ],
        compiler_params=_cparams(("parallel", "arbitrary")),
        name="moba_sample_logits",
    )(pt, qbd, k_new, bo, *([cache_k] * K_PAGES_PER_STEP))
    hps = -(-H // n_steps)
    assert H % hps == 0
    n_soft = H // hps
    nr = hps * T
    nxt = lambda s: jnp.minimum(s + 1, ndb - 1)
    part = lambda g: jnp.minimum(g, n_soft - 1)
    return pl.pallas_call(
        functools.partial(_moba_sample_pv_kernel, n_past_blocks=n_past_blocks, heads_per_step=hps),
        grid_spec=pltpu.PrefetchScalarGridSpec(
            num_scalar_prefetch=1,
            grid=(ndb, n_steps),
            in_specs=[
                pl.BlockSpec((1, n_past_blocks, rows, bs), lambda s, g, pt: (0, 0, 0, 0)),
                pl.BlockSpec((1, rows, LANES), lambda s, g, pt: (0, 0, 0)),
                pl.BlockSpec((1, rows, LANES), lambda s, g, pt: (0, 0, 0)),
                pl.BlockSpec((1, n_past_blocks, nr, bs), lambda s, g, pt: (nxt(s), 0, part(g), 0)),
                pl.BlockSpec((1, nr, LANES), lambda s, g, pt: (nxt(s), part(g), 0)),
                pl.BlockSpec((1, nr, LANES), lambda s, g, pt: (nxt(s), part(g), 0)),
                const((rows, bs)), const((rows, bs)),
                pl.BlockSpec((1, T, H * Dh), lambda s, g, pt: (s, 0, 0)),
            ] + _page_specs(n_pages, V_PAGES_PER_STEP),
            out_specs=pl.BlockSpec((1, T, H * Dh), lambda s, g, pt: (s, 0, 0)),
            scratch_shapes=[pltpu.VMEM((H, T, Dh), F32), pltpu.VMEM((2, n_past_blocks, rows, bs), F32),
                            pltpu.VMEM((2, rows, LANES), F32), pltpu.VMEM((2, rows, LANES), F32)],
        ),
        out_shape=jax.ShapeDtypeStruct((ndb, T, H * Dh), BF16),
        compiler_params=_cparams(("arbitrary", "arbitrary")),
        name="moba_sample_pv",
    )(pt, logits, scores, l_own, logits, scores, l_own, b0, slb, v_new, *([cache_v] * V_PAGES_PER_STEP))


def _merge_kernel(x_ref, yg_ref, za_ref, yb_ref, zb_ref, ga_ref, gb_ref,
                  wglu_ref, wbs_ref, wba_ref, wout_ref, gf_ref, o_ref, ya_scr, *, t_chunk):
    n_slabs = ya_scr.shape[0]
    nc = yg_ref.shape[1]
    for t in range(t_chunk):
        for s in range(n_slabs):
            rows_t = jnp.concatenate(
                [yg_ref[s * GROUPS_PER_SLAB + gl, :, t * SSM_GROUP:(t + 1) * SSM_GROUP].astype(F32)
                 for gl in range(GROUPS_PER_SLAB)], axis=1)
            ya_scr[s, pl.ds(t, nc, stride=t_chunk), :] = rows_t
    ya = jnp.concatenate([ya_scr[s] for s in range(n_slabs)], axis=1)
    s = _gelu_tanh(ya).astype(BF16)
    glu = jnp.dot(s, wglu_ref[...], preferred_element_type=F32).astype(BF16)
    s = s * _sigmoid(glu) * _silu(za_ref[...])
    a = yb_ref[...] * _silu(zb_ref[...])
    mixed = (_sigmoid(ga_ref[...]) * jnp.dot(s, wbs_ref[...], preferred_element_type=F32).astype(BF16)
             + _sigmoid(gb_ref[...]) * jnp.dot(a, wba_ref[...], preferred_element_type=F32).astype(BF16))
    o = x_ref[...] + jnp.dot(mixed, wout_ref[...], preferred_element_type=F32)
    ms = jnp.mean(o * o, axis=-1, keepdims=True)
    o_ref[...] = o * lax.rsqrt(ms + NORM_EPS) * gf_ref[...]


def _merge(x, yg, za, yb, zb, ga, gb, wglu, wbs, wba, wout, gf, *, tm, t_chunk):
    n, D = x.shape
    G, _, th = yg.shape
    assert n % tm == 0 and tm % t_chunk == 0 and th == t_chunk * SSM_GROUP
    row = pl.BlockSpec((tm, D), lambda i: (i, 0))
    wspec = pl.BlockSpec((D, D), lambda i: (0, 0), pipeline_mode=pl.Buffered(1))
    return pl.pallas_call(
        functools.partial(_merge_kernel, t_chunk=t_chunk),
        grid=(n // tm,),
        in_specs=[row, pl.BlockSpec((G, tm // t_chunk, th), lambda i: (0, i, 0))] + [row] * 5 + [wspec] * 4
        + [pl.BlockSpec((1, D), lambda i: (0, 0))],
        out_specs=row,
        out_shape=jax.ShapeDtypeStruct((n, D), F32),
        scratch_shapes=[pltpu.VMEM((D // LANES, tm, LANES), F32)],
        compiler_params=_cparams(("parallel",)),
        name="merge",
    )(x, yg, za, yb, zb, ga, gb, wglu, wbs, wba, wout, gf.reshape(1, D))


def kernel(x_prompt, x_sample, cache_k, cache_v, page_table, state_ssm_re, state_ssm_im, norm_g, w_in,
           ssm_lambda_re, ssm_lambda_im, ssm_log_dt, ssm_b_re, ssm_b_im, ssm_c_re, ssm_c_im, ssm_d,
           w_glu, w_branch_ssm, w_branch_att, w_out, final_norm_g):
    depth = w_in.shape[0]
    assert depth == 1, "single-layer trunk"
    B, L, D = x_prompt.shape
    DB, T, _ = x_sample.shape
    past_len = page_table.shape[1] * PAGE_SIZE
    slopes = 2.0 ** (-8.0 * jnp.arange(1, N_HEADS + 1, dtype=F32) / N_HEADS)
    l = 0
    w_in_b = w_in[l].astype(BF16)
    wglu, wbs, wba, wout = (w[l].astype(BF16) for w in (w_glu, w_branch_ssm, w_branch_att, w_out))
    s5_params = (ssm_lambda_re[l], ssm_lambda_im[l], ssm_log_dt[l], ssm_b_re[l], ssm_b_im[l],
                 ssm_c_re[l], ssm_c_im[l], ssm_d[l])
    G = ssm_lambda_re.shape[1]

    t_chunk = 16
    xg, za, qt, k, v, kb, vt, km, zb, ga, gb = _in_proj(x_prompt, norm_g[l], w_in_b, tm=256, t_chunk=t_chunk,
                                                        prompt=True)
    zeros = jnp.zeros((B, G, SSM_STATE), F32)
    yg, hr_p, hi_p = _s5_mix(xg, _s5_weights(*s5_params, t_chunk), zeros, zeros, t_chunk)
    yb = _moba_prompt(qt, km, kb, vt, slopes)
    n = B * L
    r2 = lambda t: t.reshape(n, D)
    y_prompt = _merge(r2(x_prompt), yg, r2(za), r2(yb), r2(zb), r2(ga), r2(gb),
                      wglu, wbs, wba, wout, final_norm_g, tm=512, t_chunk=t_chunk).reshape(B, L, D)

    ns = DB * T
    xgs, zas, qs, ks, vs, zbs, gas, gbs = _in_proj(x_sample.reshape(1, ns, D), norm_g[l], w_in_b, tm=256,
                                                   t_chunk=T, prompt=False)
    heads = lambda t: t.reshape(N_HEADS, DB, T, HEAD_DIM).transpose(1, 0, 2, 3)
    qs, ks, vs = heads(qs), heads(ks), heads(vs)
    nat = lambda t: t.transpose(0, 2, 1, 3).reshape(DB, T, N_HEADS * HEAD_DIM)
    ygs, hr_s, hi_s = _s5_mix(xgs, _s5_weights(*s5_params, T), state_ssm_re[l], state_ssm_im[l], T)
    ybs = _moba_sample(qs, nat(ks), nat(vs), cache_k, cache_v, page_table, slopes, past_len)
    r2s = lambda t: t.reshape(ns, D)
    y_sample = _merge(r2s(x_sample), ygs, r2s(zas), r2s(ybs), r2s(zbs), r2s(gas), r2s(gbs),
                      wglu, wbs, wba, wout, final_norm_g, tm=256, t_chunk=T).reshape(DB, T, D)

    return (y_prompt, y_sample, k[None], v[None], hr_p[None], hi_p[None], ks[None], vs[None], hr_s[None], hi_s[None])
```
